```python
import jax
import jax.numpy as jnp
from jax import lax
import numpy as np

D_MODEL = 1024
BATCH = 4
SEQ = 8192
DEPTH = 1

CHUNK = 64
NORM_EPS = 1e-6

RWKV_HEAD = 64
RWKV_WIDTH = D_MODEL
RWKV_HEADS = RWKV_WIDTH // RWKV_HEAD
DECAY_LORA = 64
AAA_LORA = 64
GATE_LORA = 128
RWKV_GN_EPS = 64e-5
RWKV_SPLITS = (RWKV_WIDTH, 2 * RWKV_WIDTH, 3 * RWKV_WIDTH, 3 * RWKV_WIDTH + DECAY_LORA, 3 * RWKV_WIDTH + DECAY_LORA + AAA_LORA)
RWKV_COLS = 3 * RWKV_WIDTH + DECAY_LORA + AAA_LORA + GATE_LORA

SSD_WIDTH = 2 * D_MODEL
SSD_HEAD = 64
SSD_HEADS = SSD_WIDTH // SSD_HEAD
SSD_GROUPS = 4
SSD_HEADS_PER_GROUP = SSD_HEADS // SSD_GROUPS
SSD_STATE = 128
SSD_CONV = 4
SSD_CONV_CH = SSD_WIDTH + 2 * SSD_GROUPS * SSD_STATE
SSD_COLS = SSD_WIDTH + SSD_CONV_CH + SSD_HEADS

N_BRANCH = 2
GATE_COLS = N_BRANCH * D_MODEL
IN_COLS = RWKV_COLS + SSD_COLS + GATE_COLS

N_GROUPS = 4
EXP_PER_GROUP = 8
N_EXPERTS = N_GROUPS * EXP_PER_GROUP
TOP_K = 2
D_EXPERT = 512
MOE_BLOCK = 128

kernel_name = 'hybrid_rwkv7_ssd_hmoe_block'


def rmsnorm(x, g):
    xf = x.astype(jnp.float32)
    y = xf * lax.rsqrt(jnp.mean(xf * xf, axis=-1, keepdims=True) + NORM_EPS)
    return (y * g.astype(jnp.float32)).astype(x.dtype)


def token_shift(u):
    return jnp.pad(u[:, :-1], ((0, 0), (1, 0), (0, 0)))


def rwkv7_mix(p, mu, w0, w_decay, a0, w_a, w_g, k_k, k_a, r_k, ln_g, ln_b):
    B_, S_, _ = p.shape
    f32 = jnp.float32
    u = p + mu * (token_shift(p) - p)
    r, k, v, zw, za, zg = jnp.split(u, RWKV_SPLITS, axis=-1)
    w = -jax.nn.softplus(-(w0 + jnp.tanh(zw) @ w_decay)) - 0.5
    decay = jnp.exp(-jnp.exp(w.astype(f32)))
    a = jax.nn.sigmoid(a0 + za @ w_a)
    g = jax.nn.sigmoid(zg) @ w_g

    def heads(t):
        return t.astype(f32).reshape(B_, S_, RWKV_HEADS, RWKV_HEAD)

    kk = heads(k * k_k)
    kk = kk / jnp.maximum(jnp.sqrt(jnp.sum(kk * kk, axis=-1, keepdims=True)), 1e-12)
    k = k * (1.0 + (a - 1.0) * k_a)
    r_h, k_h, v_h, a_h, w_h = heads(r), heads(k), heads(v), heads(a), heads(decay)

    def step(state, inp):
        r_t, w_t, k_t, v_t, kk_t, a_t = inp
        sa = jnp.einsum('bhij,bhj->bhi', state, -kk_t)
        state = (state * w_t[:, :, None, :]
                 + sa[..., None] * (kk_t * a_t)[:, :, None, :]
                 + v_t[..., None] * k_t[:, :, None, :])
        y_t = jnp.einsum('bhij,bhj->bhi', state, r_t)
        return state, y_t

    xs = tuple(jnp.moveaxis(t, 1, 0) for t in (r_h, w_h, k_h, v_h, kk, a_h))
    state0 = jnp.zeros((B_, RWKV_HEADS, RWKV_HEAD, RWKV_HEAD), f32)
    _, ys = lax.scan(step, state0, xs)
    y = jnp.moveaxis(ys, 0, 1)

    mean = jnp.mean(y, axis=-1, keepdims=True)
    var = jnp.mean(jnp.square(y - mean), axis=-1, keepdims=True)
    y = ((y - mean) * lax.rsqrt(var + RWKV_GN_EPS)).reshape(B_, S_, RWKV_WIDTH)
    y = y * ln_g + ln_b
    bonus = jnp.sum(r_h * k_h * r_k, axis=-1, keepdims=True) * v_h
    out = (y + bonus.reshape(B_, S_, RWKV_WIDTH)) * g
    return out.astype(p.dtype)


def causal_dwconv(u, w, b):
    c = u.shape[-1]
    y = lax.conv_general_dilated(u, w[:, None, :], window_strides=(1,), padding=[(SSD_CONV - 1, 0)],
                                 dimension_numbers=('NWC', 'WIO', 'NWC'), feature_group_count=c)
    return y + b


def ssd_chunked(xs, dt, A, Bm, Cm):
    f32 = jnp.float32
    B_, S_, _ = xs.shape
    nc = S_ // CHUNK
    G, Hg, P, N = SSD_GROUPS, SSD_HEADS_PER_GROUP, SSD_HEAD, SSD_STATE
    x = xs.astype(f32).reshape(B_, nc, CHUNK, G, Hg, P)
    dt = dt.reshape(B_, nc, CHUNK, G, Hg)
    Bc = Bm.astype(f32).reshape(B_, nc, CHUNK, G, N)
    Cc = Cm.astype(f32).reshape(B_, nc, CHUNK, G, N)
    a_dt = dt * A.reshape(G, Hg)
    a_cs = jnp.cumsum(a_dt, axis=2)
    xdt = x * dt[..., None]

    seg = a_cs[:, :, :, None] - a_cs[:, :, None, :]
    tril = jnp.tril(jnp.ones((CHUNK, CHUNK), dtype=bool))[None, None, :, :, None, None]
    decay_in = jnp.exp(jnp.where(tril, seg, -jnp.inf))
    cb = jnp.einsum('bclgn,bcsgn->bclsg', Cc, Bc)
    y_diag = jnp.einsum('bclsgh,bcsghp->bclghp', cb[..., None] * decay_in, xdt)

    decay_st = jnp.exp(a_cs[:, :, -1:] - a_cs)
    states = jnp.einsum('bcsgn,bcsghp->bcghpn', Bc, xdt * decay_st[..., None])
    chunk_decay = jnp.exp(a_cs[:, :, -1])

    def step(h, inp):
        s_c, d_c = inp
        return h * d_c[..., None, None] + s_c, h

    h0 = jnp.zeros((B_, G, Hg, P, N), f32)
    _, prev = lax.scan(step, h0, (jnp.moveaxis(states, 1, 0), jnp.moveaxis(chunk_decay, 1, 0)))
    prev = jnp.moveaxis(prev, 0, 1)
    y_off = jnp.einsum('bclgn,bcghpn->bclghp', Cc, prev) * jnp.exp(a_cs)[..., None]
    return (y_diag + y_off).reshape(B_, S_, SSD_WIDTH)


def ssd_mix(p, conv_w, conv_b, dt_bias, a_log, d_skip, norm_g):
    B_, S_, _ = p.shape
    f32 = jnp.float32
    z = p[..., :SSD_WIDTH]
    xbc = p[..., SSD_WIDTH:SSD_WIDTH + SSD_CONV_CH]
    dt_raw = p[..., SSD_WIDTH + SSD_CONV_CH:]
    xbc = jax.nn.silu(causal_dwconv(xbc, conv_w, conv_b))
    xs = xbc[..., :SSD_WIDTH]
    Bm = xbc[..., SSD_WIDTH:SSD_WIDTH + SSD_GROUPS * SSD_STATE]
    Cm = xbc[..., SSD_WIDTH + SSD_GROUPS * SSD_STATE:]
    dt = jax.nn.softplus(dt_raw.astype(f32) + dt_bias.astype(f32))
    A = -jnp.exp(a_log.astype(f32))
    y = ssd_chunked(xs, dt, A, Bm, Cm)
    y = y + (xs.astype(f32).reshape(B_, S_, SSD_HEADS, SSD_HEAD) * d_skip[:, None]).reshape(B_, S_, SSD_WIDTH)
    u = y * jax.nn.silu(z.astype(f32))
    u = u.reshape(B_, S_, SSD_GROUPS, SSD_WIDTH // SSD_GROUPS)
    u = u * lax.rsqrt(jnp.mean(u * u, axis=-1, keepdims=True) + NORM_EPS)
    u = u.reshape(B_, S_, SSD_WIDTH) * norm_g
    return u.astype(p.dtype)


def hier_moe(h, w_rg, b_rg, w_re, b_re, w_gate, w_up, w_down):
    B_, S_, D = h.shape
    T = B_ * S_
    f32 = jnp.float32
    ht = h.reshape(T, D)
    g_logits = (ht @ w_rg + b_rg).astype(f32)
    g_prob = jax.nn.softmax(g_logits, axis=-1)
    grp = jnp.argmax(g_logits, axis=-1)
    g_w = jnp.take_along_axis(g_prob, grp[:, None], axis=1)[:, 0]
    e_logits = (ht @ w_re + b_re).astype(f32).reshape(T, N_GROUPS, EXP_PER_GROUP)
    e_logits = jnp.take_along_axis(e_logits, grp[:, None, None], axis=1)[:, 0]
    e_prob = jax.nn.softmax(e_logits, axis=-1)
    top_p, top_i = lax.top_k(e_prob, TOP_K)
    gate = g_w[:, None] * top_p / jnp.sum(top_p, axis=-1, keepdims=True)
    expert = grp[:, None] * EXP_PER_GROUP + top_i

    n_assign = T * TOP_K
    e_flat = expert.reshape(-1)
    tok_flat = jnp.repeat(jnp.arange(T, dtype=jnp.int32), TOP_K)
    w_flat = gate.reshape(-1)
    order = jnp.argsort(e_flat)
    e_sorted, tok_sorted, w_sorted = e_flat[order], tok_flat[order], w_flat[order]
    counts = jnp.bincount(e_flat, length=N_EXPERTS)
    starts = jnp.cumsum(counts) - counts
    pad_counts = (counts + MOE_BLOCK - 1) // MOE_BLOCK * MOE_BLOCK
    pad_ends = jnp.cumsum(pad_counts)
    pad_starts = pad_ends - pad_counts
    dest = pad_starts[e_sorted] + (jnp.arange(n_assign) - starts[e_sorted])
    n_blocks = n_assign // MOE_BLOCK + N_EXPERTS
    buf_tok = jnp.zeros((n_blocks * MOE_BLOCK,), jnp.int32).at[dest].set(tok_sorted)
    buf_w = jnp.zeros((n_blocks * MOE_BLOCK,), f32).at[dest].set(w_sorted)
    block_exp = jnp.minimum(jnp.searchsorted(pad_ends, jnp.arange(n_blocks) * MOE_BLOCK, side='right'), N_EXPERTS - 1)

    def run_block(args):
        tok, e = args
        xb = ht[tok]
        hid = jax.nn.silu(xb @ w_gate[e]) * (xb @ w_up[e])
        return hid @ w_down[e]

    y_blocks = lax.map(run_block, (buf_tok.reshape(n_blocks, MOE_BLOCK), block_exp))
    y = jnp.zeros((T, D), f32).at[buf_tok].add(y_blocks.reshape(-1, D).astype(f32) * buf_w[:, None])
    return y.reshape(B_, S_, D).astype(h.dtype)


def setup_inputs(seed: int = 0) -> dict:
    key = jax.random.key(seed)
    ks = iter(jax.random.split(key, 40))
    f32 = jnp.float32
    L = DEPTH

    def nrm(shape, scale):
        return jax.random.normal(next(ks), shape, f32) * scale

    def unif(shape, lo, hi):
        return jax.random.uniform(next(ks), shape, f32, lo, hi)

    x = nrm((BATCH, SEQ, D_MODEL), 1.0)
    attn_norm_g = 1.0 + nrm((L, D_MODEL), 0.05)
    w_in = nrm((L, D_MODEL, IN_COLS), D_MODEL ** -0.5)
    b_gate = nrm((L, GATE_COLS), 0.1)
    rwkv_mu = unif((L, RWKV_COLS), 0.2, 0.8)
    rwkv_w0 = unif((L, RWKV_WIDTH), -6.0, -1.0)
    rwkv_w_decay = nrm((L, DECAY_LORA, RWKV_WIDTH), 0.5 * DECAY_LORA ** -0.5)
    rwkv_a0 = nrm((L, RWKV_WIDTH), 0.5)
    rwkv_w_a = nrm((L, AAA_LORA, RWKV_WIDTH), AAA_LORA ** -0.5)
    rwkv_w_g = nrm((L, GATE_LORA, RWKV_WIDTH), GATE_LORA ** -0.5)
    rwkv_k_k = 0.85 + nrm((L, RWKV_WIDTH), 0.05)
    rwkv_k_a = 1.0 + nrm((L, RWKV_WIDTH), 0.05)
    rwkv_r_k = nrm((L, RWKV_HEADS, RWKV_HEAD), 0.1)
    rwkv_ln_g = 1.0 + nrm((L, RWKV_WIDTH), 0.05)
    rwkv_ln_b = nrm((L, RWKV_WIDTH), 0.05)
    w_up_rwkv = nrm((L, RWKV_WIDTH, D_MODEL), RWKV_WIDTH ** -0.5)
    ssd_conv_w = nrm((L, SSD_CONV, SSD_CONV_CH), SSD_CONV ** -0.5)
    ssd_conv_b = nrm((L, SSD_CONV_CH), 0.05)
    dt0 = jnp.exp(unif((L, SSD_HEADS), float(np.log(1e-3)), float(np.log(1e-1))))
    ssd_dt_bias = dt0 + jnp.log(-jnp.expm1(-dt0))
    ssd_a_log = jnp.log(unif((L, SSD_HEADS), 1.0, 16.0))
    ssd_d = 1.0 + nrm((L, SSD_HEADS), 0.1)
    ssd_norm_g = 1.0 + nrm((L, SSD_WIDTH), 0.05)
    w_up_ssd = nrm((L, SSD_WIDTH, D_MODEL), SSD_WIDTH ** -0.5)
    w_out = nrm((L, D_MODEL, D_MODEL), D_MODEL ** -0.5)
    ffn_norm_g = 1.0 + nrm((L, D_MODEL), 0.05)
    w_router_group = nrm((L, D_MODEL, N_GROUPS), D_MODEL ** -0.5)
    b_router_group = nrm((L, N_GROUPS), 0.01)
    w_router_expert = nrm((L, D_MODEL, N_EXPERTS), D_MODEL ** -0.5)
    b_router_expert = nrm((L, N_EXPERTS), 0.01)
    w_exp_gate = nrm((L, N_EXPERTS, D_MODEL, D_EXPERT), D_MODEL ** -0.5)
    w_exp_up = nrm((L, N_EXPERTS, D_MODEL, D_EXPERT), D_MODEL ** -0.5)
    w_exp_down = nrm((L, N_EXPERTS, D_EXPERT, D_MODEL), D_EXPERT ** -0.5)
    final_norm_g = 1.0 + nrm((D_MODEL,), 0.05)
    return {'x': x, 'attn_norm_g': attn_norm_g, 'w_in': w_in, 'b_gate': b_gate,
            'rwkv_mu': rwkv_mu, 'rwkv_w0': rwkv_w0, 'rwkv_w_decay': rwkv_w_decay, 'rwkv_a0': rwkv_a0,
            'rwkv_w_a': rwkv_w_a, 'rwkv_w_g': rwkv_w_g, 'rwkv_k_k': rwkv_k_k, 'rwkv_k_a': rwkv_k_a,
            'rwkv_r_k': rwkv_r_k, 'rwkv_ln_g': rwkv_ln_g, 'rwkv_ln_b': rwkv_ln_b, 'w_up_rwkv': w_up_rwkv,
            'ssd_conv_w': ssd_conv_w, 'ssd_conv_b': ssd_conv_b, 'ssd_dt_bias': ssd_dt_bias,
            'ssd_a_log': ssd_a_log, 'ssd_d': ssd_d, 'ssd_norm_g': ssd_norm_g, 'w_up_ssd': w_up_ssd,
            'w_out': w_out, 'ffn_norm_g': ffn_norm_g, 'w_router_group': w_router_group,
            'b_router_group': b_router_group, 'w_router_expert': w_router_expert,
            'b_router_expert': b_router_expert, 'w_exp_gate': w_exp_gate, 'w_exp_up': w_exp_up,
            'w_exp_down': w_exp_down, 'final_norm_g': final_norm_g}


def reference(x, attn_norm_g, w_in, b_gate, rwkv_mu, rwkv_w0, rwkv_w_decay, rwkv_a0, rwkv_w_a, rwkv_w_g,
              rwkv_k_k, rwkv_k_a, rwkv_r_k, rwkv_ln_g, rwkv_ln_b, w_up_rwkv, ssd_conv_w, ssd_conv_b,
              ssd_dt_bias, ssd_a_log, ssd_d, ssd_norm_g, w_up_ssd, w_out, ffn_norm_g, w_router_group,
              b_router_group, w_router_expert, b_router_expert, w_exp_gate, w_exp_up, w_exp_down,
              final_norm_g):
    for l in range(DEPTH):
        h = rmsnorm(x, attn_norm_g[l])
        proj = h @ w_in[l]
        p_rwkv = proj[..., :RWKV_COLS]
        p_ssd = proj[..., RWKV_COLS:RWKV_COLS + SSD_COLS]
        p_gate = proj[..., RWKV_COLS + SSD_COLS:] + b_gate[l]
        y_a = rwkv7_mix(p_rwkv, rwkv_mu[l], rwkv_w0[l], rwkv_w_decay[l], rwkv_a0[l], rwkv_w_a[l], rwkv_w_g[l],
                        rwkv_k_k[l], rwkv_k_a[l], rwkv_r_k[l], rwkv_ln_g[l], rwkv_ln_b[l])
        y_b = ssd_mix(p_ssd, ssd_conv_w[l], ssd_conv_b[l], ssd_dt_bias[l], ssd_a_log[l], ssd_d[l], ssd_norm_g[l])
        gates = jax.nn.sigmoid(p_gate)
        g_a = gates[..., :D_MODEL]
        g_b = gates[..., D_MODEL:]
        merged = g_a * (y_a @ w_up_rwkv[l]) + g_b * (y_b @ w_up_ssd[l])
        x = x + merged @ w_out[l]
        h2 = rmsnorm(x, ffn_norm_g[l])
        x = x + hier_moe(h2, w_router_group[l], b_router_group[l], w_router_expert[l], b_router_expert[l],
                         w_exp_gate[l], w_exp_up[l], w_exp_down[l])
    return rmsnorm(x, final_norm_g)
```

```python
import functools

import jax
import jax.numpy as jnp
from jax import lax
from jax.experimental import pallas as pl
from jax.experimental.pallas import tpu as pltpu

F32 = jnp.float32
BF16 = jnp.bfloat16

D_MODEL = 1024
NORM_EPS = 1e-6
LANES = 128
HEAD = 64
PAIR = 2 * HEAD

RW = 1024
RW_PAIRS = RW // PAIR
DECAY_LORA = 64
AAA_LORA = 64
GATE_LORA = 128
RW_COLS = 3 * RW + DECAY_LORA + AAA_LORA + GATE_LORA
RW_GN_EPS = 64e-5
RW_CHUNK = 64
RW_TB = 256

SW = 2048
S_HEADS = SW // HEAD
S_PAIRS = SW // PAIR
S_GROUPS = 4
S_STATE = 128
S_CONV = 4
S_XBC = SW + 2 * S_GROUPS * S_STATE
S_COLS_PAD = SW + S_XBC + LANES
S_CHUNK = 128
S_TB = 256
PAIRS_PER_GROUP = S_PAIRS // S_GROUPS

N_GROUPS = 4
EPG = 8
N_EXP = 32
D_EXP = 512
MOE_ROWS = 128
MOE_TILE = 2048
MERGE_TM = 256
ROUTE_E0 = N_GROUPS

VMEM_LIMIT = 56 * 1024 * 1024


def _dot(a, b):
    return jnp.dot(a.astype(BF16), b.astype(BF16), preferred_element_type=F32)


def _dot_nt(a, b):
    return lax.dot_general(a.astype(BF16), b.astype(BF16), (((1,), (1,)), ((), ())),
                           preferred_element_type=F32)


def _split(x):
    hi = x.astype(BF16)
    lo = (x - hi.astype(F32)).astype(BF16)
    return hi, lo


def _dot_sel_l(sel, x):
    hi, lo = _split(x)
    return (jnp.dot(sel, hi, preferred_element_type=F32)
            + jnp.dot(sel, lo, preferred_element_type=F32))


def _dot_sel_r(x, sel):
    hi, lo = _split(x)
    return (jnp.dot(hi, sel, preferred_element_type=F32)
            + jnp.dot(lo, sel, preferred_element_type=F32))


def _iota(shape, dim):
    return lax.broadcasted_iota(jnp.int32, shape, dim)


def _softplus(x):
    return jnp.maximum(x, 0.0) + jnp.log(1.0 + jnp.exp(-jnp.abs(x)))


def _sigmoid(x):
    return 1.0 / (1.0 + jnp.exp(-x))


def _rms(x, g):
    return x * lax.rsqrt(jnp.mean(x * x, axis=-1, keepdims=True) + NORM_EPS) * g


def _block_diag(y):
    first = _iota(y.shape, 1) < HEAD
    zero = jnp.zeros_like(y)
    return jnp.concatenate([jnp.where(first, y, zero), jnp.where(first, zero, y)], axis=0).astype(BF16)


def _seg_sum(x, seg_ones):
    outs = []
    w = seg_ones.shape[0]
    for c in range(x.shape[1] // w):
        outs.append(_dot_sel_r(x[:, c * w:(c + 1) * w], seg_ones))
    return jnp.concatenate(outs, axis=1)


def _rwkv_kernel(x_ref, ng_ref, w_ref, mu_ref, w0_ref, wl_ref, a0_ref, wg_ref, kk_ref, ka_ref, rk_ref,
                 lng_ref, lnb_ref, o_ref,
                 prev_ref, st_ref, r_s, k_s, kk_s, b_s, v_s, ld_s, y_s):
    tb = x_ref.shape[0]
    L = RW_CHUNK

    @pl.when(pl.program_id(1) == 0)
    def _():
        prev_ref[...] = jnp.zeros_like(prev_ref)
        st_ref[...] = jnp.zeros_like(st_ref)

    h = _rms(x_ref[...], ng_ref[...])
    p = jnp.dot(h.astype(BF16), w_ref[...], preferred_element_type=F32)

    row = _iota(p.shape, 0)
    shifted = jnp.where(row == 0, prev_ref[7:8, :], pltpu.roll(p, 1, 0))
    prev_ref[...] = p[tb - 8:tb, :]
    u = p + mu_ref[...] * (shifted - p)

    r = u[:, 0:RW]
    k = u[:, RW:2 * RW]
    v = u[:, 2 * RW:3 * RW]
    zwa = u[:, 3 * RW:3 * RW + LANES]
    zg = u[:, 3 * RW + LANES:3 * RW + 2 * LANES]
    lora_in = jnp.where(_iota(zwa.shape, 1) < DECAY_LORA, jnp.tanh(zwa), zwa)
    dl = _dot(lora_in, wl_ref[...])
    wlog = -_softplus(-(w0_ref[...] + dl[:, :RW])) - 0.5
    ld = -jnp.exp(wlog)
    a = _sigmoid(a0_ref[...] + dl[:, RW:])
    gate = _dot(_sigmoid(zg), wg_ref[...])

    seg_ones = (_iota((2 * PAIR, 2 * PAIR), 0) // HEAD == _iota((2 * PAIR, 2 * PAIR), 1) // HEAD).astype(BF16)
    kk = k * kk_ref[...]
    kk = kk / jnp.maximum(jnp.sqrt(_seg_sum(kk * kk, seg_ones)), 1e-12)
    k2 = k * (1.0 + (a - 1.0) * ka_ref[...])
    bonus = _seg_sum(r * k2 * rk_ref[...], seg_ones) * v

    r_s[...] = r
    k_s[...] = k2
    kk_s[...] = kk
    b_s[...] = kk * a
    v_s[...] = v
    ld_s[...] = ld

    t_i = _iota((L, PAIR), 0)
    s_i = _iota((L, PAIR), 1) % HEAD
    strict = s_i < t_i
    incl = s_i <= t_i
    eye = (s_i == t_i).astype(F32)
    tril_l = (_iota((L, L), 1) <= _iota((L, L), 0)).astype(BF16)
    bd_mask = (_iota((PAIR, PAIR), 0) < HEAD) == (_iota((PAIR, PAIR), 1) < HEAD)

    def level_mask(b):
        same = (t_i // (2 * b)) == (s_i // (2 * b))
        return same & ((t_i // b) % 2 == 1) & ((s_i // b) % 2 == 0)

    def chunk(c, carry):
        rows = pl.ds(pl.multiple_of(c * L, L), L)
        ldc = ld_s[rows, :]
        cs = _dot_sel_l(tril_l, ldc)
        cs_last = cs[L - 1:L, :]
        e_in = jnp.exp(cs)
        e_ex = jnp.exp(cs - ldc)
        e_inv = jnp.exp(-cs)
        e_end = jnp.exp(cs_last - cs)
        g_end = jnp.exp(cs_last)
        kkc = kk_s[rows, :]
        bc = b_s[rows, :]
        kc = k_s[rows, :]
        a_t = -kkc * e_ex
        r_t = r_s[rows, :] * e_in
        b_t = bc * e_inv
        k_t = kc * e_inv
        b_h = bc * e_end
        k_h = kc * e_end
        vc = v_s[rows, :]

        for pr in range(RW_PAIRS):
            sl = slice(pr * PAIR, (pr + 1) * PAIR)
            a_p, r_p, v_p = a_t[:, sl], r_t[:, sl], vc[:, sl]
            lhs = jnp.concatenate([a_p, r_p], axis=0)
            rhs = jnp.concatenate([_block_diag(b_t[:, sl]), _block_diag(k_t[:, sl])], axis=0)
            g = _dot_nt(lhs, rhs)
            n_ab = jnp.where(strict, g[0:L, 0:PAIR], 0.0)
            a_ak = jnp.where(strict, g[0:L, PAIR:], 0.0)
            m_rb = jnp.where(incl, g[L:, 0:PAIR], 0.0)
            m_rk = jnp.where(incl, g[L:, PAIR:], 0.0)

            xinv = eye + jnp.where(level_mask(1), n_ab, 0.0)
            b = 2
            while b < L:
                nb = jnp.where(level_mask(b), n_ab, 0.0)
                q = _dot(_dot(xinv, _block_diag(nb)), _block_diag(xinv))
                xinv = xinv + q
                b *= 2

            v_bd = _block_diag(v_p)
            akv = _dot(a_ak, v_bd)
            wu = _dot(xinv, jnp.concatenate([_block_diag(a_p), _block_diag(akv)], axis=1))
            w_t = wu[:, 0:PAIR]
            y_loc = _dot(m_rk, v_bd)

            s_bd = st_ref[pr]
            u_c = _dot_nt(w_t, s_bd) + wu[:, PAIR:]
            y_c = _dot_nt(r_p, s_bd) + _dot(m_rb, _block_diag(u_c)) + y_loc
            uv = jnp.concatenate([u_c, v_p], axis=0)
            bk = jnp.concatenate([b_h[:, sl], k_h[:, sl]], axis=0)
            upd = _dot(uv.T, bk)
            st_ref[pr] = s_bd * g_end[:, sl] + jnp.where(bd_mask, upd, 0.0)
            y_s[rows, sl] = y_c
        return carry

    lax.fori_loop(0, tb // L, chunk, 0)

    y = y_s[...]
    mean = _seg_sum(y, seg_ones) * (1.0 / HEAD)
    yc = y - mean
    var = _seg_sum(yc * yc, seg_ones) * (1.0 / HEAD)
    yn = yc * lax.rsqrt(var + RW_GN_EPS) * lng_ref[...] + lnb_ref[...]
    o_ref[...] = ((yn + bonus) * gate).astype(o_ref.dtype)


def _rwkv_call(x, ng, w, mu, w0, wl, a0, wg, k_k, k_a, r_k, ln_g, ln_b):
    B, S, _ = x.shape
    tb = RW_TB
    full = lambda shape: pl.BlockSpec(shape, lambda b, i: (0,) * len(shape))
    row = full((1, RW))
    return pl.pallas_call(
        _rwkv_kernel,
        grid=(B, S // tb),
        in_specs=[pl.BlockSpec((None, tb, D_MODEL), lambda b, i: (b, i, 0)),
                  full((1, D_MODEL)), full((D_MODEL, RW_COLS)), full((1, RW_COLS)), row,
                  full((LANES, 2 * RW)), row, full((GATE_LORA, RW)), row, row, row, row, row],
        out_specs=pl.BlockSpec((None, tb, RW), lambda b, i: (b, i, 0)),
        out_shape=jax.ShapeDtypeStruct((B, S, RW), BF16),
        scratch_shapes=[pltpu.VMEM((8, RW_COLS), F32), pltpu.VMEM((RW_PAIRS, PAIR, PAIR), F32)]
        + [pltpu.VMEM((tb, RW), F32)] * 7,
        compiler_params=pltpu.CompilerParams(dimension_semantics=("arbitrary", "arbitrary"),
                                             vmem_limit_bytes=VMEM_LIMIT),
        name="rwkv",
    )(x, ng, w, mu, w0, wl, a0, wg, k_k, k_a, r_k, ln_g, ln_b)


def _ssd_kernel(x_ref, ng_ref, w_ref, cw_ref, cb_ref, dtb_ref, alog_ref, dx_ref, sng_ref, exp_ref, o_ref,
                prev_ref, st_ref, xs_s, xdt_s, b_s, c_s, acs_s, acst_s, y_s):
    tb = x_ref.shape[0]
    L = S_CHUNK

    @pl.when(pl.program_id(1) == 0)
    def _():
        prev_ref[...] = jnp.zeros_like(prev_ref)
        st_ref[...] = jnp.zeros_like(st_ref)

    h = _rms(x_ref[...], ng_ref[...])
    p = jnp.dot(h.astype(BF16), w_ref[...], preferred_element_type=F32)
    z = p[:, 0:SW]
    xbc = p[:, SW:SW + S_XBC]
    dt_raw = p[:, SW + S_XBC:]

    carry = prev_ref[...]
    prev_ref[...] = xbc[tb - 8:tb, :]
    row8 = _iota((8, S_XBC), 0)
    cw = cw_ref[...]
    conv = cb_ref[...] + cw[S_CONV - 1:S_CONV, :] * xbc
    for j in range(1, S_CONV):
        sh = pltpu.roll(xbc, j, 0)
        top = jnp.where(row8 < j, pltpu.roll(carry, j, 0), sh[0:8, :])
        sh = jnp.concatenate([top, sh[8:, :]], axis=0)
        conv = conv + cw[S_CONV - 1 - j:S_CONV - j, :] * sh
    act = conv * _sigmoid(conv)
    xs = act[:, 0:SW]
    b_s[...] = act[:, SW:SW + S_GROUPS * S_STATE]
    c_s[...] = act[:, SW + S_GROUPS * S_STATE:]

    head_lane = _iota((tb, LANES), 1) < S_HEADS
    dt = jnp.where(head_lane, _softplus(dt_raw + dtb_ref[...]), 0.0)
    a_dt = dt * (-jnp.exp(alog_ref[...]))
    expand = exp_ref[...]
    xdt = xs * _dot_sel_r(dt, expand)
    xs_s[...] = xs
    xdt_s[...] = xdt

    ri = _iota((tb, tb), 0)
    ci = _iota((tb, tb), 1)
    same_chunk = (ri // L) == (ci // L)
    tril_blk = (same_chunk & (ci <= ri)).astype(BF16)
    triu_blk = (same_chunk & (ri <= ci)).astype(BF16)
    acs = _dot_sel_l(tril_blk, a_dt)
    acs_s[...] = acs
    acst = _dot_sel_r(a_dt.T, triu_blk)
    for c in range(tb // L):
        acst_s[c] = acst[:, c * L:(c + 1) * L]

    tril_ll = _iota((L, L), 1) <= _iota((L, L), 0)

    def chunk(c, carry_):
        r0 = pl.multiple_of(c * L, L)
        rows = pl.ds(r0, L)
        acs_c = acs_s[rows, :]
        acs_t = acst_s[c]
        a_last = acs_c[L - 1:L, :]
        e_acs = _dot_sel_r(jnp.exp(acs_c), expand)
        e_st = _dot_sel_r(jnp.exp(a_last - acs_c), expand)
        e_end = _dot_sel_r(jnp.broadcast_to(jnp.exp(a_last), (8, LANES)), expand)[0:1, :]
        xdt_c = xdt_s[rows, :]
        xw = xdt_c * e_st
        for g in range(S_GROUPS):
            gs = slice(g * S_STATE, (g + 1) * S_STATE)
            b_g = b_s[rows, gs]
            c_g = c_s[rows, gs]
            cb = _dot_nt(c_g, b_g)
            b_gt = b_g.T
            for q in range(PAIRS_PER_GROUP):
                pr = g * PAIRS_PER_GROUP + q
                sl = slice(pr * PAIR, (pr + 1) * PAIR)
                ms = []
                for hh in (2 * pr, 2 * pr + 1):
                    seg = acs_c[:, hh:hh + 1] - acs_t[hh:hh + 1, :]
                    ms.append(cb * jnp.exp(jnp.where(tril_ll, seg, -jnp.inf)))
                y_d = _dot(jnp.concatenate(ms, axis=1), _block_diag(xdt_c[:, sl]))
                s_p = st_ref[pr]
                y_o = _dot(c_g, s_p) * e_acs[:, sl]
                st_ref[pr] = s_p * e_end[:, sl] + _dot(b_gt, xw[:, sl])
                y_s[rows, sl] = y_d + y_o
        return carry_

    lax.fori_loop(0, tb // L, chunk, 0)

    y = y_s[...] + xs * dx_ref[...]
    uu = y * (z * _sigmoid(z))
    gw = SW // S_GROUPS
    outs = []
    for g in range(S_GROUPS):
        ug = uu[:, g * gw:(g + 1) * gw]
        outs.append(ug * lax.rsqrt(jnp.mean(ug * ug, axis=-1, keepdims=True) + NORM_EPS))
    o_ref[...] = (jnp.concatenate(outs, axis=1) * sng_ref[...]).astype(o_ref.dtype)


def _ssd_call(x, ng, w, cw, cb, dtb, alog, dx, sng, expand):
    B, S, _ = x.shape
    tb = S_TB
    full = lambda shape: pl.BlockSpec(shape, lambda b, i: (0,) * len(shape))
    return pl.pallas_call(
        _ssd_kernel,
        grid=(B, S // tb),
        in_specs=[pl.BlockSpec((None, tb, D_MODEL), lambda b, i: (b, i, 0)),
                  full((1, D_MODEL)), full((D_MODEL, S_COLS_PAD)), full((S_CONV, S_XBC)), full((1, S_XBC)),
                  full((1, LANES)), full((1, LANES)), full((1, SW)), full((1, SW)), full((LANES, SW))],
        out_specs=pl.BlockSpec((None, tb, SW), lambda b, i: (b, i, 0)),
        out_shape=jax.ShapeDtypeStruct((B, S, SW), BF16),
        scratch_shapes=[pltpu.VMEM((8, S_XBC), F32), pltpu.VMEM((S_PAIRS, S_STATE, PAIR), F32),
                        pltpu.VMEM((tb, SW), F32), pltpu.VMEM((tb, SW), F32),
                        pltpu.VMEM((tb, S_GROUPS * S_STATE), F32), pltpu.VMEM((tb, S_GROUPS * S_STATE), F32),
                        pltpu.VMEM((tb, LANES), F32), pltpu.VMEM((tb // S_CHUNK, LANES, S_CHUNK), F32),
                        pltpu.VMEM((tb, SW), F32)],
        compiler_params=pltpu.CompilerParams(dimension_semantics=("arbitrary", "arbitrary"),
                                             vmem_limit_bytes=VMEM_LIMIT),
        name="ssd",
    )(x, ng, w, cw, cb, dtb, alog, dx, sng, expand)


def _merge_kernel(x_ref, ya_ref, yb_ref, ng_ref, wgt_ref, bgt_ref, wa_ref, wb_ref, wo_ref, fg_ref,
                  wrh_ref, wrl_ref, br_ref, x1_ref, h2_ref, rt_ref, cnt_ref, carry_ref, *, tiles_per_moe):
    tm = x_ref.shape[0]
    i = pl.program_id(0)

    @pl.when(i % tiles_per_moe == 0)
    def _():
        carry_ref[...] = jnp.zeros_like(carry_ref)

    x = x_ref[...]
    h = _rms(x, ng_ref[...])
    gates = _sigmoid(jnp.dot(h.astype(BF16), wgt_ref[...], preferred_element_type=F32) + bgt_ref[...])
    up_a = jnp.dot(ya_ref[...], wa_ref[...], preferred_element_type=F32)
    up_b = jnp.dot(yb_ref[...], wb_ref[...], preferred_element_type=F32)
    merged = gates[:, :D_MODEL] * up_a + gates[:, D_MODEL:] * up_b
    x1 = x + _dot(merged, wo_ref[...])
    x1_ref[...] = x1
    h2 = _rms(x1, fg_ref[...])
    h2_ref[...] = h2

    hh, hl = _split(h2)
    wh, wl = wrh_ref[...], wrl_ref[...]
    logits = (jnp.dot(hh, wh, preferred_element_type=F32) + jnp.dot(hh, wl, preferred_element_type=F32)
              + jnp.dot(hl, wh, preferred_element_type=F32)) + br_ref[...]
    li = _iota(logits.shape, 1).astype(F32)
    neg = -jnp.inf
    big = float(LANES)

    gl = jnp.where(li < N_GROUPS, logits, neg)
    gmax = jnp.max(gl, axis=-1, keepdims=True)
    grp = jnp.min(jnp.where(gl == gmax, li, big), axis=-1, keepdims=True)
    g_w = 1.0 / jnp.sum(jnp.exp(gl - gmax), axis=-1, keepdims=True)

    lo_lane = ROUTE_E0 + EPG * grp
    el = jnp.where((li >= lo_lane) & (li < lo_lane + EPG), logits, neg)
    emax = jnp.max(el, axis=-1, keepdims=True)
    i1 = jnp.min(jnp.where(el == emax, li, big), axis=-1, keepdims=True)
    esum = jnp.sum(jnp.exp(el - emax), axis=-1, keepdims=True)
    el2 = jnp.where(li == i1, neg, el)
    m2 = jnp.max(el2, axis=-1, keepdims=True)
    i2 = jnp.min(jnp.where(el2 == m2, li, big), axis=-1, keepdims=True)
    p1 = 1.0 / esum
    p2 = jnp.exp(m2 - emax) / esum
    gate1 = g_w * p1 / (p1 + p2)
    gate2 = g_w * p2 / (p1 + p2)

    hit1 = li == i1
    hit2 = li == i2
    onehot = jnp.where(hit1 | hit2, 1.0, 0.0)
    tril_strict = (_iota((tm, tm), 1) < _iota((tm, tm), 0)).astype(BF16)
    cum = jnp.dot(tril_strict, onehot.astype(BF16), preferred_element_type=F32) + carry_ref[0:1, :]
    r1 = jnp.sum(jnp.where(hit1, cum, 0.0), axis=-1, keepdims=True)
    r2 = jnp.sum(jnp.where(hit2, cum, 0.0), axis=-1, keepdims=True)
    carry_ref[...] = carry_ref[...] + jnp.sum(onehot, axis=0, keepdims=True)
    cnt_ref[...] = carry_ref[...]

    e1 = i1 - ROUTE_E0
    e2 = i2 - ROUTE_E0
    rt = jnp.zeros(logits.shape, F32)
    for lane, val in enumerate((e1, e2, r1, r2, gate1, gate2)):
        rt = jnp.where(li == lane, val, rt)
    rt_ref[...] = rt


def _merge_call(x2, ya, yb, ng, wgt, bgt, wa, wb, wo, fg, wrh, wrl, br, moe_tile):
    T = x2.shape[0]
    tm = MERGE_TM
    tiles_per_moe = moe_tile // tm
    n_moe = T // moe_tile
    full = lambda shape: pl.BlockSpec(shape, lambda i: (0,) * len(shape))
    rows = lambda w: pl.BlockSpec((tm, w), lambda i: (i, 0))
    return pl.pallas_call(
        functools.partial(_merge_kernel, tiles_per_moe=tiles_per_moe),
        grid=(T // tm,),
        in_specs=[rows(D_MODEL), rows(RW), rows(SW), full((1, D_MODEL)), full((D_MODEL, 2 * D_MODEL)),
                  full((1, 2 * D_MODEL)), full((RW, D_MODEL)), full((SW, D_MODEL)), full((D_MODEL, D_MODEL)),
                  full((1, D_MODEL)), full((D_MODEL, LANES)), full((D_MODEL, LANES)), full((1, LANES))],
        out_specs=[rows(D_MODEL), rows(D_MODEL), rows(LANES),
                   pl.BlockSpec((None, 8, LANES), lambda i: (i // tiles_per_moe, 0, 0))],
        out_shape=[jax.ShapeDtypeStruct((T, D_MODEL), F32), jax.ShapeDtypeStruct((T, D_MODEL), F32),
                   jax.ShapeDtypeStruct((T, LANES), F32), jax.ShapeDtypeStruct((n_moe, 8, LANES), F32)],
        scratch_shapes=[pltpu.VMEM((8, LANES), F32)],
        compiler_params=pltpu.CompilerParams(dimension_semantics=("arbitrary",),
                                             vmem_limit_bytes=VMEM_LIMIT),
        name="merge",
    )(x2, ya, yb, ng, wgt, bgt, wa, wb, wo, fg, wrh, wrl, br)


def _moe_kernel(idx_ref, gts_ref, cnt_ref, h2_ref, wg_ref, wu_ref, wd_ref, y_ref,
                start_ref, stok_ref, sg_ref, xb_ref, ob_ref):
    e = pl.program_id(1)
    tt = h2_ref.shape[0]

    @pl.when(e == 0)
    def _():
        def starts(j, acc):
            start_ref[j] = acc
            return acc + cnt_ref[0, j]
        lax.fori_loop(0, N_EXP, starts, jnp.int32(0))

        def place(t, c):
            for k in range(2):
                slot = start_ref[idx_ref[0, k * tt + t]] + idx_ref[0, (2 + k) * tt + t]
                stok_ref[slot] = t
                sg_ref[slot] = gts_ref[0, k * tt + t]
            return c
        lax.fori_loop(0, tt, place, 0)
        y_ref[...] = jnp.zeros_like(y_ref)
        xb_ref[...] = jnp.zeros_like(xb_ref)

    n = cnt_ref[0, e]
    s0 = start_ref[e]

    def block(b, c):
        base = s0 + b * MOE_ROWS
        m = jnp.minimum(n - b * MOE_ROWS, MOE_ROWS)

        def gather(r, c2):
            t = stok_ref[base + r]
            xb_ref[pl.ds(r, 1), :] = h2_ref[pl.ds(t, 1), :]
            return c2
        lax.fori_loop(0, m, gather, 0)

        xb = xb_ref[...].astype(BF16)
        hg = jnp.dot(xb, wg_ref[...], preferred_element_type=F32)
        hu = jnp.dot(xb, wu_ref[...], preferred_element_type=F32)
        hid = hg * _sigmoid(hg) * hu
        ob_ref[...] = jnp.dot(hid.astype(BF16), wd_ref[...], preferred_element_type=F32)

        def scatter(r, c2):
            t = stok_ref[base + r]
            y_ref[pl.ds(t, 1), :] = y_ref[pl.ds(t, 1), :] + sg_ref[base + r] * ob_ref[pl.ds(r, 1), :]
            return c2
        lax.fori_loop(0, m, scatter, 0)
        return c

    lax.fori_loop(0, (n + MOE_ROWS - 1) // MOE_ROWS, block, 0)


def _moe_call(idx, gts, cnt, h2, wg, wu, wd, moe_tile):
    T = h2.shape[0]
    tt = moe_tile
    smem = lambda w: pl.BlockSpec((None, 1, w), lambda i, e: (i, 0, 0), memory_space=pltpu.SMEM)
    return pl.pallas_call(
        _moe_kernel,
        grid=(T // tt, N_EXP),
        in_specs=[smem(4 * tt), smem(2 * tt), smem(N_EXP),
                  pl.BlockSpec((tt, D_MODEL), lambda i, e: (i, 0)),
                  pl.BlockSpec((None, D_MODEL, D_EXP), lambda i, e: (e, 0, 0)),
                  pl.BlockSpec((None, D_MODEL, D_EXP), lambda i, e: (e, 0, 0)),
                  pl.BlockSpec((None, D_EXP, D_MODEL), lambda i, e: (e, 0, 0))],
        out_specs=pl.BlockSpec((tt, D_MODEL), lambda i, e: (i, 0)),
        out_shape=jax.ShapeDtypeStruct((T, D_MODEL), F32),
        scratch_shapes=[pltpu.SMEM((N_EXP,), jnp.int32), pltpu.SMEM((2 * tt,), jnp.int32),
                        pltpu.SMEM((2 * tt,), F32), pltpu.VMEM((MOE_ROWS, D_MODEL), F32),
                        pltpu.VMEM((MOE_ROWS, D_MODEL), F32)],
        compiler_params=pltpu.CompilerParams(dimension_semantics=("arbitrary", "arbitrary"),
                                             vmem_limit_bytes=VMEM_LIMIT),
        name="moe",
    )(idx, gts, cnt, h2, wg, wu, wd)


def _final_kernel(x1_ref, y_ref, g_ref, o_ref):
    o_ref[...] = _rms(x1_ref[...] + y_ref[...], g_ref[...])


def _final_call(x1, y, g):
    T = x1.shape[0]
    tm = 512
    rows = pl.BlockSpec((tm, D_MODEL), lambda i: (i, 0))
    return pl.pallas_call(
        _final_kernel,
        grid=(T // tm,),
        in_specs=[rows, rows, pl.BlockSpec((1, D_MODEL), lambda i: (0, 0))],
        out_specs=rows,
        out_shape=jax.ShapeDtypeStruct((T, D_MODEL), F32),
        compiler_params=pltpu.CompilerParams(dimension_semantics=("arbitrary",)),
        name="final",
    )(x1, y, g)


def _layer(x, attn_norm_g, w_in, b_gate, rwkv_mu, rwkv_w0, rwkv_w_decay, rwkv_a0, rwkv_w_a, rwkv_w_g,
           rwkv_k_k, rwkv_k_a, rwkv_r_k, rwkv_ln_g, rwkv_ln_b, w_up_rwkv, ssd_conv_w, ssd_conv_b,
           ssd_dt_bias, ssd_a_log, ssd_d, ssd_norm_g, w_up_ssd, w_out, ffn_norm_g, w_router_group,
           b_router_group, w_router_expert, b_router_expert, w_exp_gate, w_exp_up, w_exp_down):
    B, S, _ = x.shape
    T = B * S
    row = lambda a: a.reshape(1, -1).astype(F32)
    ng = row(attn_norm_g)

    w_rw = w_in[:, :RW_COLS].astype(BF16)
    wl = jnp.zeros((LANES, 2 * RW), F32)
    wl = wl.at[:DECAY_LORA, :RW].set(rwkv_w_decay).at[DECAY_LORA:, RW:].set(rwkv_w_a).astype(BF16)
    y_a = _rwkv_call(x, ng, w_rw, row(rwkv_mu), row(rwkv_w0), wl, row(rwkv_a0), rwkv_w_g.astype(BF16),
                     row(rwkv_k_k), row(rwkv_k_a), row(rwkv_r_k), row(rwkv_ln_g), row(rwkv_ln_b))

    n_ssd = 2 * SW + 2 * S_GROUPS * S_STATE + S_HEADS
    w_ssd = jnp.pad(w_in[:, RW_COLS:RW_COLS + n_ssd], ((0, 0), (0, S_COLS_PAD - n_ssd))).astype(BF16)
    pad_h = lambda a: jnp.pad(a.reshape(1, -1).astype(F32), ((0, 0), (0, LANES - S_HEADS)))
    expand = (jnp.arange(LANES)[:, None] == (jnp.arange(SW) // HEAD)[None, :]).astype(BF16)
    y_b = _ssd_call(x, ng, w_ssd, ssd_conv_w.astype(F32), row(ssd_conv_b), pad_h(ssd_dt_bias), pad_h(ssd_a_log),
                    row(jnp.repeat(ssd_d, HEAD)), row(ssd_norm_g), expand)

    moe_tile = min(MOE_TILE, T)
    w_gt = w_in[:, RW_COLS + n_ssd:].astype(BF16)
    w_r = jnp.zeros((D_MODEL, LANES), F32)
    w_r = w_r.at[:, :N_GROUPS].set(w_router_group).at[:, ROUTE_E0:ROUTE_E0 + N_EXP].set(w_router_expert)
    w_rh = w_r.astype(BF16)
    w_rl = (w_r - w_rh.astype(F32)).astype(BF16)
    b_r = jnp.zeros((1, LANES), F32)
    b_r = b_r.at[0, :N_GROUPS].set(b_router_group).at[0, ROUTE_E0:ROUTE_E0 + N_EXP].set(b_router_expert)
    x1, h2, route, cnt = _merge_call(
        x.reshape(T, D_MODEL), y_a.reshape(T, RW), y_b.reshape(T, SW), ng, w_gt, row(b_gate),
        w_up_rwkv.astype(BF16), w_up_ssd.astype(BF16), w_out.astype(BF16), row(ffn_norm_g), w_rh, w_rl, b_r,
        moe_tile)

    n_moe = T // moe_tile
    idx = route[:, 0:4].astype(jnp.int32).reshape(n_moe, moe_tile, 4).transpose(0, 2, 1).reshape(n_moe, 1, 4 * moe_tile)
    gts = route[:, 4:6].reshape(n_moe, moe_tile, 2).transpose(0, 2, 1).reshape(n_moe, 1, 2 * moe_tile)
    counts = cnt[:, 0:1, ROUTE_E0:ROUTE_E0 + N_EXP].astype(jnp.int32)
    y_moe = _moe_call(idx, gts, counts, h2, w_exp_gate.astype(BF16), w_exp_up.astype(BF16),
                      w_exp_down.astype(BF16), moe_tile)
    return x1, y_moe


def kernel(x, attn_norm_g, w_in, b_gate, rwkv_mu, rwkv_w0, rwkv_w_decay, rwkv_a0, rwkv_w_a, rwkv_w_g, rwkv_k_k, rwkv_k_a, rwkv_r_k, rwkv_ln_g, rwkv_ln_b, w_up_rwkv, ssd_conv_w, ssd_conv_b, ssd_dt_bias, ssd_a_log, ssd_d, ssd_norm_g, w_up_ssd, w_out, ffn_norm_g, w_router_group, b_router_group, w_router_expert, b_router_expert, w_exp_gate, w_exp_up, w_exp_down, final_norm_g):
    B, S, _ = x.shape
    depth = attn_norm_g.shape[0]
    assert depth == 1, "the final residual add is fused with the last layer's MoE output"
    layer_params = (attn_norm_g, w_in, b_gate, rwkv_mu, rwkv_w0, rwkv_w_decay, rwkv_a0, rwkv_w_a, rwkv_w_g,
                    rwkv_k_k, rwkv_k_a, rwkv_r_k, rwkv_ln_g, rwkv_ln_b, w_up_rwkv, ssd_conv_w, ssd_conv_b,
                    ssd_dt_bias, ssd_a_log, ssd_d, ssd_norm_g, w_up_ssd, w_out, ffn_norm_g, w_router_group,
                    b_router_group, w_router_expert, b_router_expert, w_exp_gate, w_exp_up, w_exp_down)
    x1, y_moe = _layer(x, *(prm[0] for prm in layer_params))
    out = _final_call(x1, y_moe, final_norm_g.reshape(1, -1).astype(F32))
    return out.reshape(B, S, D_MODEL)
```

```python
import functools

import jax
import jax.numpy as jnp
from jax import lax
from jax.experimental import pallas as pl
from jax.experimental.pallas import tpu as pltpu

F32 = jnp.float32
BF16 = jnp.bfloat16

D_MODEL = 1024
NORM_EPS = 1e-6
LANES = 128
HEAD = 64
PAIR = 2 * HEAD

RW = 1024
RW_PAIRS = RW // PAIR
DECAY_LORA = 64
AAA_LORA = 64
GATE_LORA = 128
RW_COLS = 3 * RW + DECAY_LORA + AAA_LORA + GATE_LORA
RW_GN_EPS = 64e-5
RW_CHUNK = 64
RW_TB = 256

SW = 2048
S_HEADS = SW // HEAD
S_PAIRS = SW // PAIR
S_GROUPS = 4
S_STATE = 128
S_CONV = 4
S_XBC = SW + 2 * S_GROUPS * S_STATE
S_COLS_PAD = SW + S_XBC + LANES
S_CHUNK = 128
S_TB = 256
PAIRS_PER_GROUP = S_PAIRS // S_GROUPS

N_GROUPS = 4
EPG = 8
N_EXP = 32
D_EXP = 512
MOE_ROWS = 128
MOE_TILE = 2048
MOE_UNROLL = 8
MERGE_TM = 256
ROUTE_E0 = N_GROUPS

VMEM_LIMIT = 56 * 1024 * 1024


def _dot(a, b):
    return jnp.dot(a.astype(BF16), b.astype(BF16), preferred_element_type=F32)


def _dot_nt(a, b):
    return lax.dot_general(a.astype(BF16), b.astype(BF16), (((1,), (1,)), ((), ())),
                           preferred_element_type=F32)


def _split(x):
    hi = x.astype(BF16)
    lo = (x - hi.astype(F32)).astype(BF16)
    return hi, lo


def _dot_sel_l(sel, x):
    hi, lo = _split(x)
    return (jnp.dot(sel, hi, preferred_element_type=F32)
            + jnp.dot(sel, lo, preferred_element_type=F32))


def _dot_sel_r(x, sel):
    hi, lo = _split(x)
    return (jnp.dot(hi, sel, preferred_element_type=F32)
            + jnp.dot(lo, sel, preferred_element_type=F32))


def _iota(shape, dim):
    return lax.broadcasted_iota(jnp.int32, shape, dim)


def _softplus(x):
    return jnp.maximum(x, 0.0) + jnp.log(1.0 + jnp.exp(-jnp.abs(x)))


def _sigmoid(x):
    return 1.0 / (1.0 + jnp.exp(-x))


def _rms(x, g):
    return x * lax.rsqrt(jnp.mean(x * x, axis=-1, keepdims=True) + NORM_EPS) * g


def _block_diag(y):
    first = _iota(y.shape, 1) < HEAD
    zero = jnp.zeros_like(y)
    return jnp.concatenate([jnp.where(first, y, zero), jnp.where(first, zero, y)], axis=0).astype(BF16)


def _seg_sum(x, seg_ones):
    outs = []
    w = seg_ones.shape[0]
    for c in range(x.shape[1] // w):
        outs.append(_dot_sel_r(x[:, c * w:(c + 1) * w], seg_ones))
    return jnp.concatenate(outs, axis=1)


def _rwkv_kernel(x_ref, ng_ref, w_ref, mu_ref, w0_ref, wl_ref, a0_ref, wg_ref, kk_ref, ka_ref, rk_ref,
                 lng_ref, lnb_ref, o_ref,
                 prev_ref, st_ref, r_s, k_s, kk_s, b_s, v_s, ld_s, y_s):
    tb = x_ref.shape[0]
    L = RW_CHUNK

    @pl.when(pl.program_id(1) == 0)
    def _():
        prev_ref[...] = jnp.zeros_like(prev_ref)
        st_ref[...] = jnp.zeros_like(st_ref)

    h = _rms(x_ref[...], ng_ref[...])
    p = jnp.dot(h.astype(BF16), w_ref[...], preferred_element_type=F32)

    row = _iota(p.shape, 0)
    shifted = jnp.where(row == 0, prev_ref[7:8, :], pltpu.roll(p, 1, 0))
    prev_ref[...] = p[tb - 8:tb, :]
    u = p + mu_ref[...] * (shifted - p)

    r = u[:, 0:RW]
    k = u[:, RW:2 * RW]
    v = u[:, 2 * RW:3 * RW]
    zwa = u[:, 3 * RW:3 * RW + LANES]
    zg = u[:, 3 * RW + LANES:3 * RW + 2 * LANES]
    lora_in = jnp.where(_iota(zwa.shape, 1) < DECAY_LORA, jnp.tanh(zwa), zwa)
    dl = _dot(lora_in, wl_ref[...])
    wlog = -_softplus(-(w0_ref[...] + dl[:, :RW])) - 0.5
    ld = -jnp.exp(wlog)
    a = _sigmoid(a0_ref[...] + dl[:, RW:])
    gate = _dot(_sigmoid(zg), wg_ref[...])

    seg_ones = (_iota((2 * PAIR, 2 * PAIR), 0) // HEAD == _iota((2 * PAIR, 2 * PAIR), 1) // HEAD).astype(BF16)
    kk = k * kk_ref[...]
    kk = kk / jnp.maximum(jnp.sqrt(_seg_sum(kk * kk, seg_ones)), 1e-12)
    k2 = k * (1.0 + (a - 1.0) * ka_ref[...])
    bonus = _seg_sum(r * k2 * rk_ref[...], seg_ones) * v

    r_s[...] = r
    k_s[...] = k2
    kk_s[...] = kk
    b_s[...] = kk * a
    v_s[...] = v
    ld_s[...] = ld

    t_i = _iota((L, PAIR), 0)
    s_i = _iota((L, PAIR), 1) % HEAD
    strict = s_i < t_i
    incl = s_i <= t_i
    eye = (s_i == t_i).astype(F32)
    tril_l = (_iota((L, L), 1) <= _iota((L, L), 0)).astype(BF16)
    bd_mask = (_iota((PAIR, PAIR), 0) < HEAD) == (_iota((PAIR, PAIR), 1) < HEAD)

    def level_mask(b):
        same = (t_i // (2 * b)) == (s_i // (2 * b))
        return same & ((t_i // b) % 2 == 1) & ((s_i // b) % 2 == 0)

    def chunk(c, carry):
        rows = pl.ds(pl.multiple_of(c * L, L), L)
        ldc = ld_s[rows, :]
        cs = _dot_sel_l(tril_l, ldc)
        cs_last = cs[L - 1:L, :]
        e_in = jnp.exp(cs)
        e_ex = jnp.exp(cs - ldc)
        e_inv = jnp.exp(-cs)
        e_end = jnp.exp(cs_last - cs)
        g_end = jnp.exp(cs_last)
        kkc = kk_s[rows, :]
        bc = b_s[rows, :]
        kc = k_s[rows, :]
        a_t = -kkc * e_ex
        r_t = r_s[rows, :] * e_in
        b_t = bc * e_inv
        k_t = kc * e_inv
        b_h = bc * e_end
        k_h = kc * e_end
        vc = v_s[rows, :]

        prs = range(RW_PAIRS)
        sls = [slice(pr * PAIR, (pr + 1) * PAIR) for pr in prs]
        g = [_dot_nt(jnp.concatenate([a_t[:, sl], r_t[:, sl]], axis=0),
                     jnp.concatenate([_block_diag(b_t[:, sl]), _block_diag(k_t[:, sl])], axis=0))
             for sl in sls]
        n_ab = [jnp.where(strict, g[pr][0:L, 0:PAIR], 0.0) for pr in prs]
        a_ak = [jnp.where(strict, g[pr][0:L, PAIR:], 0.0) for pr in prs]
        m_rb = [jnp.where(incl, g[pr][L:, 0:PAIR], 0.0) for pr in prs]
        m_rk = [jnp.where(incl, g[pr][L:, PAIR:], 0.0) for pr in prs]

        lm = level_mask(1)
        xinv = [eye + jnp.where(lm, n_ab[pr], 0.0) for pr in prs]
        b = 2
        while b < L:
            lm = level_mask(b)
            t1 = [_dot(xinv[pr], _block_diag(jnp.where(lm, n_ab[pr], 0.0))) for pr in prs]
            q = [_dot(t1[pr], _block_diag(xinv[pr])) for pr in prs]
            xinv = [xinv[pr] + q[pr] for pr in prs]
            b *= 2

        v_bd = [_block_diag(vc[:, sl]) for sl in sls]
        akv = [_dot(a_ak[pr], v_bd[pr]) for pr in prs]
        wu = [_dot(xinv[pr], jnp.concatenate([_block_diag(a_t[:, sls[pr]]), _block_diag(akv[pr])], axis=1))
              for pr in prs]
        y_loc = [_dot(m_rk[pr], v_bd[pr]) for pr in prs]

        s_bd = [st_ref[pr] for pr in prs]
        u_c = [_dot_nt(wu[pr][:, 0:PAIR], s_bd[pr]) + wu[pr][:, PAIR:] for pr in prs]
        y_r = [_dot_nt(r_t[:, sls[pr]], s_bd[pr]) for pr in prs]
        y_c = [y_r[pr] + _dot(m_rb[pr], _block_diag(u_c[pr])) + y_loc[pr] for pr in prs]
        upd = [_dot(jnp.concatenate([u_c[pr], vc[:, sls[pr]]], axis=0).T,
                    jnp.concatenate([b_h[:, sls[pr]], k_h[:, sls[pr]]], axis=0)) for pr in prs]
        for pr in prs:
            st_ref[pr] = s_bd[pr] * g_end[:, sls[pr]] + jnp.where(bd_mask, upd[pr], 0.0)
            y_s[rows, sls[pr]] = y_c[pr]
        return carry

    lax.fori_loop(0, tb // L, chunk, 0)

    y = y_s[...]
    mean = _seg_sum(y, seg_ones) * (1.0 / HEAD)
    yc = y - mean
    var = _seg_sum(yc * yc, seg_ones) * (1.0 / HEAD)
    yn = yc * lax.rsqrt(var + RW_GN_EPS) * lng_ref[...] + lnb_ref[...]
    o_ref[...] = ((yn + bonus) * gate).astype(o_ref.dtype)


def _rwkv_call(x, ng, w, mu, w0, wl, a0, wg, k_k, k_a, r_k, ln_g, ln_b):
    B, S, _ = x.shape
    tb = RW_TB
    full = lambda shape: pl.BlockSpec(shape, lambda b, i: (0,) * len(shape))
    row = full((1, RW))
    return pl.pallas_call(
        _rwkv_kernel,
        grid=(B, S // tb),
        in_specs=[pl.BlockSpec((None, tb, D_MODEL), lambda b, i: (b, i, 0)),
                  full((1, D_MODEL)), full((D_MODEL, RW_COLS)), full((1, RW_COLS)), row,
                  full((LANES, 2 * RW)), row, full((GATE_LORA, RW)), row, row, row, row, row],
        out_specs=pl.BlockSpec((None, tb, RW), lambda b, i: (b, i, 0)),
        out_shape=jax.ShapeDtypeStruct((B, S, RW), BF16),
        scratch_shapes=[pltpu.VMEM((8, RW_COLS), F32), pltpu.VMEM((RW_PAIRS, PAIR, PAIR), F32)]
        + [pltpu.VMEM((tb, RW), F32)] * 7,
        compiler_params=pltpu.CompilerParams(dimension_semantics=("arbitrary", "arbitrary"),
                                             vmem_limit_bytes=VMEM_LIMIT),
        name="rwkv",
    )(x, ng, w, mu, w0, wl, a0, wg, k_k, k_a, r_k, ln_g, ln_b)


def _ssd_kernel(x_ref, ng_ref, w_ref, cw_ref, cb_ref, dtb_ref, alog_ref, dx_ref, sng_ref, exp_ref, o_ref,
                prev_ref, st_ref, xs_s, xdt_s, b_s, c_s, acs_s, acst_s, y_s):
    tb = x_ref.shape[0]
    L = S_CHUNK

    @pl.when(pl.program_id(1) == 0)
    def _():
        prev_ref[...] = jnp.zeros_like(prev_ref)
        st_ref[...] = jnp.zeros_like(st_ref)

    h = _rms(x_ref[...], ng_ref[...])
    p = jnp.dot(h.astype(BF16), w_ref[...], preferred_element_type=F32)
    z = p[:, 0:SW]
    xbc = p[:, SW:SW + S_XBC]
    dt_raw = p[:, SW + S_XBC:]

    carry = prev_ref[...]
    prev_ref[...] = xbc[tb - 8:tb, :]
    row8 = _iota((8, S_XBC), 0)
    cw = cw_ref[...]
    conv = cb_ref[...] + cw[S_CONV - 1:S_CONV, :] * xbc
    for j in range(1, S_CONV):
        sh = pltpu.roll(xbc, j, 0)
        top = jnp.where(row8 < j, pltpu.roll(carry, j, 0), sh[0:8, :])
        sh = jnp.concatenate([top, sh[8:, :]], axis=0)
        conv = conv + cw[S_CONV - 1 - j:S_CONV - j, :] * sh
    act = conv * _sigmoid(conv)
    xs = act[:, 0:SW]
    b_s[...] = act[:, SW:SW + S_GROUPS * S_STATE]
    c_s[...] = act[:, SW + S_GROUPS * S_STATE:]

    head_lane = _iota((tb, LANES), 1) < S_HEADS
    dt = jnp.where(head_lane, _softplus(dt_raw + dtb_ref[...]), 0.0)
    a_dt = dt * (-jnp.exp(alog_ref[...]))
    expand = exp_ref[...]
    xdt = xs * _dot_sel_r(dt, expand)
    xs_s[...] = xs
    xdt_s[...] = xdt

    ri = _iota((tb, tb), 0)
    ci = _iota((tb, tb), 1)
    same_chunk = (ri // L) == (ci // L)
    tril_blk = (same_chunk & (ci <= ri)).astype(BF16)
    triu_blk = (same_chunk & (ri <= ci)).astype(BF16)
    acs = _dot_sel_l(tril_blk, a_dt)
    acs_s[...] = acs
    acst = _dot_sel_r(a_dt.T, triu_blk)
    for c in range(tb // L):
        acst_s[c] = acst[:, c * L:(c + 1) * L]

    tril_ll = _iota((L, L), 1) <= _iota((L, L), 0)

    def chunk(c, carry_):
        r0 = pl.multiple_of(c * L, L)
        rows = pl.ds(r0, L)
        acs_c = acs_s[rows, :]
        acs_t = acst_s[c]
        a_last = acs_c[L - 1:L, :]
        e_acs = _dot_sel_r(jnp.exp(acs_c), expand)
        e_st = _dot_sel_r(jnp.exp(a_last - acs_c), expand)
        e_end = _dot_sel_r(jnp.broadcast_to(jnp.exp(a_last), (8, LANES)), expand)[0:1, :]
        xdt_c = xdt_s[rows, :]
        xw = xdt_c * e_st
        for g in range(S_GROUPS):
            gs = slice(g * S_STATE, (g + 1) * S_STATE)
            b_g = b_s[rows, gs]
            c_g = c_s[rows, gs]
            cb = _dot_nt(c_g, b_g)
            b_gt = b_g.T
            for q in range(PAIRS_PER_GROUP):
                pr = g * PAIRS_PER_GROUP + q
                sl = slice(pr * PAIR, (pr + 1) * PAIR)
                ms = []
                for hh in (2 * pr, 2 * pr + 1):
                    seg = acs_c[:, hh:hh + 1] - acs_t[hh:hh + 1, :]
                    ms.append(cb * jnp.exp(jnp.where(tril_ll, seg, -jnp.inf)))
                y_d = _dot(jnp.concatenate(ms, axis=1), _block_diag(xdt_c[:, sl]))
                s_p = st_ref[pr]
                y_o = _dot(c_g, s_p) * e_acs[:, sl]
                st_ref[pr] = s_p * e_end[:, sl] + _dot(b_gt, xw[:, sl])
                y_s[rows, sl] = y_d + y_o
        return carry_

    lax.fori_loop(0, tb // L, chunk, 0)

    y = y_s[...] + xs * dx_ref[...]
    uu = y * (z * _sigmoid(z))
    gw = SW // S_GROUPS
    outs = []
    for g in range(S_GROUPS):
        ug = uu[:, g * gw:(g + 1) * gw]
        outs.append(ug * lax.rsqrt(jnp.mean(ug * ug, axis=-1, keepdims=True) + NORM_EPS))
    o_ref[...] = (jnp.concatenate(outs, axis=1) * sng_ref[...]).astype(o_ref.dtype)


def _ssd_call(x, ng, w, cw, cb, dtb, alog, dx, sng, expand):
    B, S, _ = x.shape
    tb = S_TB
    full = lambda shape: pl.BlockSpec(shape, lambda b, i: (0,) * len(shape))
    return pl.pallas_call(
        _ssd_kernel,
        grid=(B, S // tb),
        in_specs=[pl.BlockSpec((None, tb, D_MODEL), lambda b, i: (b, i, 0)),
                  full((1, D_MODEL)), full((D_MODEL, S_COLS_PAD)), full((S_CONV, S_XBC)), full((1, S_XBC)),
                  full((1, LANES)), full((1, LANES)), full((1, SW)), full((1, SW)), full((LANES, SW))],
        out_specs=pl.BlockSpec((None, tb, SW), lambda b, i: (b, i, 0)),
        out_shape=jax.ShapeDtypeStruct((B, S, SW), BF16),
        scratch_shapes=[pltpu.VMEM((8, S_XBC), F32), pltpu.VMEM((S_PAIRS, S_STATE, PAIR), F32),
                        pltpu.VMEM((tb, SW), F32), pltpu.VMEM((tb, SW), F32),
                        pltpu.VMEM((tb, S_GROUPS * S_STATE), F32), pltpu.VMEM((tb, S_GROUPS * S_STATE), F32),
                        pltpu.VMEM((tb, LANES), F32), pltpu.VMEM((tb // S_CHUNK, LANES, S_CHUNK), F32),
                        pltpu.VMEM((tb, SW), F32)],
        compiler_params=pltpu.CompilerParams(dimension_semantics=("arbitrary", "arbitrary"),
                                             vmem_limit_bytes=VMEM_LIMIT),
        name="ssd",
    )(x, ng, w, cw, cb, dtb, alog, dx, sng, expand)


def _merge_kernel(x_ref, ya_ref, yb_ref, ng_ref, wgt_ref, bgt_ref, wa_ref, wb_ref, wo_ref, fg_ref,
                  wrh_ref, wrl_ref, br_ref, x1_ref, h2_ref, rt_ref, cnt_ref, carry_ref, *, tiles_per_moe):
    tm = x_ref.shape[0]
    i = pl.program_id(0)

    @pl.when(i % tiles_per_moe == 0)
    def _():
        carry_ref[...] = jnp.zeros_like(carry_ref)

    x = x_ref[...]
    h = _rms(x, ng_ref[...])
    gates = _sigmoid(jnp.dot(h.astype(BF16), wgt_ref[...], preferred_element_type=F32) + bgt_ref[...])
    up_a = jnp.dot(ya_ref[...], wa_ref[...], preferred_element_type=F32)
    up_b = jnp.dot(yb_ref[...], wb_ref[...], preferred_element_type=F32)
    merged = gates[:, :D_MODEL] * up_a + gates[:, D_MODEL:] * up_b
    x1 = x + _dot(merged, wo_ref[...])
    x1_ref[...] = x1
    h2 = _rms(x1, fg_ref[...])
    h2_ref[...] = h2

    hh, hl = _split(h2)
    wh, wl = wrh_ref[...], wrl_ref[...]
    logits = (jnp.dot(hh, wh, preferred_element_type=F32) + jnp.dot(hh, wl, preferred_element_type=F32)
              + jnp.dot(hl, wh, preferred_element_type=F32)) + br_ref[...]
    li = _iota(logits.shape, 1).astype(F32)
    neg = -jnp.inf
    big = float(LANES)

    gl = jnp.where(li < N_GROUPS, logits, neg)
    gmax = jnp.max(gl, axis=-1, keepdims=True)
    grp = jnp.min(jnp.where(gl == gmax, li, big), axis=-1, keepdims=True)
    g_w = 1.0 / jnp.sum(jnp.exp(gl - gmax), axis=-1, keepdims=True)

    lo_lane = ROUTE_E0 + EPG * grp
    el = jnp.where((li >= lo_lane) & (li < lo_lane + EPG), logits, neg)
    emax = jnp.max(el, axis=-1, keepdims=True)
    i1 = jnp.min(jnp.where(el == emax, li, big), axis=-1, keepdims=True)
    esum = jnp.sum(jnp.exp(el - emax), axis=-1, keepdims=True)
    el2 = jnp.where(li == i1, neg, el)
    m2 = jnp.max(el2, axis=-1, keepdims=True)
    i2 = jnp.min(jnp.where(el2 == m2, li, big), axis=-1, keepdims=True)
    p1 = 1.0 / esum
    p2 = jnp.exp(m2 - emax) / esum
    gate1 = g_w * p1 / (p1 + p2)
    gate2 = g_w * p2 / (p1 + p2)

    hit1 = li == i1
    hit2 = li == i2
    onehot = jnp.where(hit1 | hit2, 1.0, 0.0)
    tril_strict = (_iota((tm, tm), 1) < _iota((tm, tm), 0)).astype(BF16)
    cum = jnp.dot(tril_strict, onehot.astype(BF16), preferred_element_type=F32) + carry_ref[0:1, :]
    r1 = jnp.sum(jnp.where(hit1, cum, 0.0), axis=-1, keepdims=True)
    r2 = jnp.sum(jnp.where(hit2, cum, 0.0), axis=-1, keepdims=True)
    carry_ref[...] = carry_ref[...] + jnp.sum(onehot, axis=0, keepdims=True)
    cnt_ref[...] = carry_ref[...]

    e1 = i1 - ROUTE_E0
    e2 = i2 - ROUTE_E0
    rt = jnp.zeros(logits.shape, F32)
    for lane, val in enumerate((e1, e2, r1, r2, gate1, gate2)):
        rt = jnp.where(li == lane, val, rt)
    rt_ref[...] = rt


def _merge_call(x2, ya, yb, ng, wgt, bgt, wa, wb, wo, fg, wrh, wrl, br, moe_tile):
    T = x2.shape[0]
    tm = MERGE_TM
    tiles_per_moe = moe_tile // tm
    n_moe = T // moe_tile
    full = lambda shape: pl.BlockSpec(shape, lambda i: (0,) * len(shape))
    rows = lambda w: pl.BlockSpec((tm, w), lambda i: (i, 0))
    return pl.pallas_call(
        functools.partial(_merge_kernel, tiles_per_moe=tiles_per_moe),
        grid=(T // tm,),
        in_specs=[rows(D_MODEL), rows(RW), rows(SW), full((1, D_MODEL)), full((D_MODEL, 2 * D_MODEL)),
                  full((1, 2 * D_MODEL)), full((RW, D_MODEL)), full((SW, D_MODEL)), full((D_MODEL, D_MODEL)),
                  full((1, D_MODEL)), full((D_MODEL, LANES)), full((D_MODEL, LANES)), full((1, LANES))],
        out_specs=[rows(D_MODEL), rows(D_MODEL), rows(LANES),
                   pl.BlockSpec((None, 8, LANES), lambda i: (i // tiles_per_moe, 0, 0))],
        out_shape=[jax.ShapeDtypeStruct((T, D_MODEL), F32), jax.ShapeDtypeStruct((T, D_MODEL), F32),
                   jax.ShapeDtypeStruct((T, LANES), F32), jax.ShapeDtypeStruct((n_moe, 8, LANES), F32)],
        scratch_shapes=[pltpu.VMEM((8, LANES), F32)],
        compiler_params=pltpu.CompilerParams(dimension_semantics=("arbitrary",),
                                             vmem_limit_bytes=VMEM_LIMIT),
        name="merge",
    )(x2, ya, yb, ng, wgt, bgt, wa, wb, wo, fg, wrh, wrl, br)


def _moe_kernel(idx_ref, gts_ref, cnt_ref, h2_ref, wg_ref, wu_ref, wd_ref, y_ref,
                start_ref, stok_ref, sg_ref, xb_ref, ob_ref):
    e = pl.program_id(1)
    tt = h2_ref.shape[0]

    @pl.when(e == 0)
    def _():
        def starts(j, acc):
            start_ref[j] = acc
            return acc + cnt_ref[0, j]
        lax.fori_loop(0, N_EXP, starts, jnp.int32(0))

        def place(j, c):
            for u in range(MOE_UNROLL):
                t = j * MOE_UNROLL + u
                for k in range(2):
                    slot = start_ref[idx_ref[0, k * tt + t]] + idx_ref[0, (2 + k) * tt + t]
                    stok_ref[slot] = t
                    sg_ref[slot] = gts_ref[0, k * tt + t]
            return c
        lax.fori_loop(0, tt // MOE_UNROLL, place, 0)
        y_ref[...] = jnp.zeros_like(y_ref)
        xb_ref[...] = jnp.zeros_like(xb_ref)

    n = cnt_ref[0, e]
    s0 = start_ref[e]
    last_slot = 2 * tt - 1

    def block(b, c):
        base = s0 + b * MOE_ROWS
        m = jnp.minimum(n - b * MOE_ROWS, MOE_ROWS)
        groups = (m + MOE_UNROLL - 1) // MOE_UNROLL

        def gather(j, c2):
            for u in range(MOE_UNROLL):
                r = j * MOE_UNROLL + u
                t = stok_ref[jnp.minimum(base + r, last_slot)]
                xb_ref[pl.ds(r, 1), :] = h2_ref[pl.ds(t, 1), :]
            return c2
        lax.fori_loop(0, groups, gather, 0)

        xb = xb_ref[...].astype(BF16)
        hg = jnp.dot(xb, wg_ref[...], preferred_element_type=F32)
        hu = jnp.dot(xb, wu_ref[...], preferred_element_type=F32)
        hid = hg * _sigmoid(hg) * hu
        ob_ref[...] = jnp.dot(hid.astype(BF16), wd_ref[...], preferred_element_type=F32)

        def scatter(j, c2):
            for u in range(MOE_UNROLL):
                r = j * MOE_UNROLL + u
                slot = jnp.minimum(base + r, last_slot)
                t = stok_ref[slot]
                gate = jnp.where(r < m, sg_ref[slot], 0.0)
                y_ref[pl.ds(t, 1), :] = y_ref[pl.ds(t, 1), :] + gate * ob_ref[pl.ds(r, 1), :]
            return c2
        lax.fori_loop(0, groups, scatter, 0)
        return c

    lax.fori_loop(0, (n + MOE_ROWS - 1) // MOE_ROWS, block, 0)


def _moe_call(idx, gts, cnt, h2, wg, wu, wd, moe_tile):
    T = h2.shape[0]
    tt = moe_tile
    smem = lambda w: pl.BlockSpec((None, 1, w), lambda i, e: (i, 0, 0), memory_space=pltpu.SMEM)
    return pl.pallas_call(
        _moe_kernel,
        grid=(T // tt, N_EXP),
        in_specs=[smem(4 * tt), smem(2 * tt), smem(N_EXP),
                  pl.BlockSpec((tt, D_MODEL), lambda i, e: (i, 0)),
                  pl.BlockSpec((None, D_MODEL, D_EXP), lambda i, e: (e, 0, 0)),
                  pl.BlockSpec((None, D_MODEL, D_EXP), lambda i, e: (e, 0, 0)),
                  pl.BlockSpec((None, D_EXP, D_MODEL), lambda i, e: (e, 0, 0))],
        out_specs=pl.BlockSpec((tt, D_MODEL), lambda i, e: (i, 0)),
        out_shape=jax.ShapeDtypeStruct((T, D_MODEL), F32),
        scratch_shapes=[pltpu.SMEM((N_EXP,), jnp.int32), pltpu.SMEM((2 * tt,), jnp.int32),
                        pltpu.SMEM((2 * tt,), F32), pltpu.VMEM((MOE_ROWS, D_MODEL), F32),
                        pltpu.VMEM((MOE_ROWS, D_MODEL), F32)],
        compiler_params=pltpu.CompilerParams(dimension_semantics=("arbitrary", "arbitrary"),
                                             vmem_limit_bytes=VMEM_LIMIT),
        name="moe",
    )(idx, gts, cnt, h2, wg, wu, wd)


def _final_kernel(x1_ref, y_ref, g_ref, o_ref):
    o_ref[...] = _rms(x1_ref[...] + y_ref[...], g_ref[...])


def _final_call(x1, y, g):
    T = x1.shape[0]
    tm = 512
    rows = pl.BlockSpec((tm, D_MODEL), lambda i: (i, 0))
    return pl.pallas_call(
        _final_kernel,
        grid=(T // tm,),
        in_specs=[rows, rows, pl.BlockSpec((1, D_MODEL), lambda i: (0, 0))],
        out_specs=rows,
        out_shape=jax.ShapeDtypeStruct((T, D_MODEL), F32),
        compiler_params=pltpu.CompilerParams(dimension_semantics=("arbitrary",)),
        name="final",
    )(x1, y, g)


def _layer(x, attn_norm_g, w_in, b_gate, rwkv_mu, rwkv_w0, rwkv_w_decay, rwkv_a0, rwkv_w_a, rwkv_w_g,
           rwkv_k_k, rwkv_k_a, rwkv_r_k, rwkv_ln_g, rwkv_ln_b, w_up_rwkv, ssd_conv_w, ssd_conv_b,
           ssd_dt_bias, ssd_a_log, ssd_d, ssd_norm_g, w_up_ssd, w_out, ffn_norm_g, w_router_group,
           b_router_group, w_router_expert, b_router_expert, w_exp_gate, w_exp_up, w_exp_down):
    B, S, _ = x.shape
    T = B * S
    row = lambda a: a.reshape(1, -1).astype(F32)
    ng = row(attn_norm_g)

    w_rw = w_in[:, :RW_COLS].astype(BF16)
    wl = jnp.zeros((LANES, 2 * RW), F32)
    wl = wl.at[:DECAY_LORA, :RW].set(rwkv_w_decay).at[DECAY_LORA:, RW:].set(rwkv_w_a).astype(BF16)
    y_a = _rwkv_call(x, ng, w_rw, row(rwkv_mu), row(rwkv_w0), wl, row(rwkv_a0), rwkv_w_g.astype(BF16),
                     row(rwkv_k_k), row(rwkv_k_a), row(rwkv_r_k), row(rwkv_ln_g), row(rwkv_ln_b))

    n_ssd = 2 * SW + 2 * S_GROUPS * S_STATE + S_HEADS
    w_ssd = jnp.pad(w_in[:, RW_COLS:RW_COLS + n_ssd], ((0, 0), (0, S_COLS_PAD - n_ssd))).astype(BF16)
    pad_h = lambda a: jnp.pad(a.reshape(1, -1).astype(F32), ((0, 0), (0, LANES - S_HEADS)))
    expand = (jnp.arange(LANES)[:, None] == (jnp.arange(SW) // HEAD)[None, :]).astype(BF16)
    y_b = _ssd_call(x, ng, w_ssd, ssd_conv_w.astype(F32), row(ssd_conv_b), pad_h(ssd_dt_bias), pad_h(ssd_a_log),
                    row(jnp.repeat(ssd_d, HEAD)), row(ssd_norm_g), expand)

    moe_tile = min(MOE_TILE, T)
    w_gt = w_in[:, RW_COLS + n_ssd:].astype(BF16)
    w_r = jnp.zeros((D_MODEL, LANES), F32)
    w_r = w_r.at[:, :N_GROUPS].set(w_router_group).at[:, ROUTE_E0:ROUTE_E0 + N_EXP].set(w_router_expert)
    w_rh = w_r.astype(BF16)
    w_rl = (w_r - w_rh.astype(F32)).astype(BF16)
    b_r = jnp.zeros((1, LANES), F32)
    b_r = b_r.at[0, :N_GROUPS].set(b_router_group).at[0, ROUTE_E0:ROUTE_E0 + N_EXP].set(b_router_expert)
    x1, h2, route, cnt = _merge_call(
        x.reshape(T, D_MODEL), y_a.reshape(T, RW), y_b.reshape(T, SW), ng, w_gt, row(b_gate),
        w_up_rwkv.astype(BF16), w_up_ssd.astype(BF16), w_out.astype(BF16), row(ffn_norm_g), w_rh, w_rl, b_r,
        moe_tile)

    n_moe = T // moe_tile
    idx = route[:, 0:4].astype(jnp.int32).reshape(n_moe, moe_tile, 4).transpose(0, 2, 1).reshape(n_moe, 1, 4 * moe_tile)
    gts = route[:, 4:6].reshape(n_moe, moe_tile, 2).transpose(0, 2, 1).reshape(n_moe, 1, 2 * moe_tile)
    counts = cnt[:, 0:1, ROUTE_E0:ROUTE_E0 + N_EXP].astype(jnp.int32)
    y_moe = _moe_call(idx, gts, counts, h2, w_exp_gate.astype(BF16), w_exp_up.astype(BF16),
                      w_exp_down.astype(BF16), moe_tile)
    return x1, y_moe


def kernel(x, attn_norm_g, w_in, b_gate, rwkv_mu, rwkv_w0, rwkv_w_decay, rwkv_a0, rwkv_w_a, rwkv_w_g, rwkv_k_k, rwkv_k_a, rwkv_r_k, rwkv_ln_g, rwkv_ln_b, w_up_rwkv, ssd_conv_w, ssd_conv_b, ssd_dt_bias, ssd_a_log, ssd_d, ssd_norm_g, w_up_ssd, w_out, ffn_norm_g, w_router_group, b_router_group, w_router_expert, b_router_expert, w_exp_gate, w_exp_up, w_exp_down, final_norm_g):
    B, S, _ = x.shape
    depth = attn_norm_g.shape[0]
    assert depth == 1, "the final residual add is fused with the last layer's MoE output"
    layer_params = (attn_norm_g, w_in, b_gate, rwkv_mu, rwkv_w0, rwkv_w_decay, rwkv_a0, rwkv_w_a, rwkv_w_g,
                    rwkv_k_k, rwkv_k_a, rwkv_r_k, rwkv_ln_g, rwkv_ln_b, w_up_rwkv, ssd_conv_w, ssd_conv_b,
                    ssd_dt_bias, ssd_a_log, ssd_d, ssd_norm_g, w_up_ssd, w_out, ffn_norm_g, w_router_group,
                    b_router_group, w_router_expert, b_router_expert, w_exp_gate, w_exp_up, w_exp_down)
    x1, y_moe = _layer(x, *(prm[0] for prm in layer_params))
    out = _final_call(x1, y_moe, final_norm_g.reshape(1, -1).astype(F32))
    return out.reshape(B, S, D_MODEL)
```

```python
import functools

import jax
import jax.numpy as jnp
from jax import lax
from jax.experimental import pallas as pl
from jax.experimental.pallas import tpu as pltpu

F32 = jnp.float32
BF16 = jnp.bfloat16

D_MODEL = 1024
NORM_EPS = 1e-6
LANES = 128
HEAD = 64
PAIR = 2 * HEAD

RW = 1024
RW_PAIRS = RW // PAIR
DECAY_LORA = 64
AAA_LORA = 64
GATE_LORA = 128
RW_COLS = 3 * RW + DECAY_LORA + AAA_LORA + GATE_LORA
RW_GN_EPS = 64e-5
RW_CHUNK = 64
RW_GROUP = 4
RW_TB = 256
EXP_NEG_HALF = 0.6065306597126334

SW = 2048
S_HEADS = SW // HEAD
S_PAIRS = SW // PAIR
S_GROUPS = 4
S_STATE = 128
S_CONV = 4
S_XBC = SW + 2 * S_GROUPS * S_STATE
S_COLS_PAD = SW + S_XBC + LANES
S_CHUNK = 128
S_TB = 256
PAIRS_PER_GROUP = S_PAIRS // S_GROUPS

N_GROUPS = 4
EPG = 8
N_EXP = 32
D_EXP = 512
MOE_ROWS = 128
MOE_TILE = 2048
MOE_UNROLL = 8
MERGE_TM = 512
ROUTE_E0 = N_GROUPS

VMEM_LIMIT = 56 * 1024 * 1024


def _dot(a, b):
    return jnp.dot(a.astype(BF16), b.astype(BF16), preferred_element_type=F32)


def _dot_nt(a, b):
    return lax.dot_general(a.astype(BF16), b.astype(BF16), (((1,), (1,)), ((), ())),
                           preferred_element_type=F32)


def _split(x):
    hi = x.astype(BF16)
    lo = (x - hi.astype(F32)).astype(BF16)
    return hi, lo


def _dot_sel_l(sel, x):
    hi, lo = _split(x)
    return (jnp.dot(sel, hi, preferred_element_type=F32)
            + jnp.dot(sel, lo, preferred_element_type=F32))


def _dot_sel_r(x, sel):
    hi, lo = _split(x)
    return (jnp.dot(hi, sel, preferred_element_type=F32)
            + jnp.dot(lo, sel, preferred_element_type=F32))


def _iota(shape, dim):
    return lax.broadcasted_iota(jnp.int32, shape, dim)


def _softplus(x):
    return jnp.maximum(x, 0.0) + jnp.log(1.0 + jnp.exp(-jnp.abs(x)))


def _sigmoid(x):
    return 1.0 / (1.0 + jnp.exp(-x))


def _rms(x, g):
    return x * lax.rsqrt(jnp.mean(x * x, axis=-1, keepdims=True) + NORM_EPS) * g


def _block_diag(y):
    first = _iota(y.shape, 1) < HEAD
    zero = jnp.zeros_like(y)
    return jnp.concatenate([jnp.where(first, y, zero), jnp.where(first, zero, y)], axis=0).astype(BF16)


def _seg_sum(x, seg_ones):
    outs = []
    w = seg_ones.shape[0]
    for c in range(x.shape[1] // w):
        outs.append(_dot_sel_r(x[:, c * w:(c + 1) * w], seg_ones))
    return jnp.concatenate(outs, axis=1)


def _rwkv_kernel(x_ref, ng_ref, w_ref, mu_ref, w0_ref, wl_ref, a0_ref, wg_ref, kk_ref, ka_ref, rk_ref,
                 lng_ref, lnb_ref, o_ref,
                 prev_ref, st_ref, r_s, k_s, kk_s, b_s, v_s, ld_s, y_s):
    tb = x_ref.shape[0]
    L = RW_CHUNK

    @pl.when(pl.program_id(1) == 0)
    def _():
        prev_ref[...] = jnp.zeros_like(prev_ref)
        st_ref[...] = jnp.zeros_like(st_ref)

    h = _rms(x_ref[...], ng_ref[...])
    p = jnp.dot(h.astype(BF16), w_ref[...], preferred_element_type=F32)

    rolled = pltpu.roll(p, 1, 0)
    top = jnp.where(_iota((8, RW_COLS), 0) == 0, prev_ref[7:8, :], rolled[0:8, :])
    shifted = jnp.concatenate([top, rolled[8:, :]], axis=0)
    prev_ref[...] = p[tb - 8:tb, :]
    u = p + mu_ref[...] * (shifted - p)

    r = u[:, 0:RW]
    k = u[:, RW:2 * RW]
    v = u[:, 2 * RW:3 * RW]
    zwa = u[:, 3 * RW:3 * RW + LANES]
    zg = u[:, 3 * RW + LANES:3 * RW + 2 * LANES]
    lora_in = jnp.where(_iota(zwa.shape, 1) < DECAY_LORA, jnp.tanh(zwa), zwa)
    dl = _dot(lora_in, wl_ref[...])
    ld = -EXP_NEG_HALF * _sigmoid(w0_ref[...] + dl[:, :RW])
    a = _sigmoid(a0_ref[...] + dl[:, RW:])
    gate = _dot(_sigmoid(zg), wg_ref[...])

    seg_ones = (_iota((2 * PAIR, 2 * PAIR), 0) // HEAD == _iota((2 * PAIR, 2 * PAIR), 1) // HEAD).astype(BF16)
    kk = k * kk_ref[...]
    kk = kk * lax.rsqrt(jnp.maximum(_seg_sum(kk * kk, seg_ones), 1e-24))
    k2 = k * (1.0 + (a - 1.0) * ka_ref[...])
    bonus = _seg_sum(r * k2 * rk_ref[...], seg_ones) * v

    r_s[...] = r
    k_s[...] = k2
    kk_s[...] = kk
    b_s[...] = kk * a
    v_s[...] = v
    ld_s[...] = ld

    t_i = _iota((L, PAIR), 0)
    s_i = _iota((L, PAIR), 1) % HEAD
    strict = s_i < t_i
    incl = s_i <= t_i
    eye = (s_i == t_i).astype(F32)
    tril_l = (_iota((L, L), 1) <= _iota((L, L), 0)).astype(BF16)
    bd_mask = (_iota((PAIR, PAIR), 0) < HEAD) == (_iota((PAIR, PAIR), 1) < HEAD)

    def level_mask(b):
        same = (t_i // (2 * b)) == (s_i // (2 * b))
        return same & ((t_i // b) % 2 == 1) & ((s_i // b) % 2 == 0)

    prs = range(RW_PAIRS)
    sls = [slice(pr * PAIR, (pr + 1) * PAIR) for pr in prs]

    def group(cg, carry):
        rows, a_t, r_t, b_t, k_t, b_h, k_h, vc, g_end = [], [], [], [], [], [], [], [], []
        for j in range(RW_GROUP):
            rj = pl.ds(pl.multiple_of((cg * RW_GROUP + j) * L, L), L)
            ldc = ld_s[rj, :]
            cs = _dot_sel_l(tril_l, ldc)
            cs_last = cs[L - 1:L, :]
            e_inv = jnp.exp(-cs)
            e_end = jnp.exp(cs_last - cs)
            bc = b_s[rj, :]
            kc = k_s[rj, :]
            rows.append(rj)
            a_t.append(-kk_s[rj, :] * jnp.exp(cs - ldc))
            r_t.append(r_s[rj, :] * jnp.exp(cs))
            b_t.append(bc * e_inv)
            k_t.append(kc * e_inv)
            b_h.append(bc * e_end)
            k_h.append(kc * e_end)
            vc.append(v_s[rj, :])
            g_end.append(jnp.exp(cs_last))

        items = [(j, pr) for j in range(RW_GROUP) for pr in prs]
        g = [_dot_nt(jnp.concatenate([a_t[j][:, sls[pr]], r_t[j][:, sls[pr]]], axis=0),
                     jnp.concatenate([_block_diag(b_t[j][:, sls[pr]]), _block_diag(k_t[j][:, sls[pr]])], axis=0))
             for j, pr in items]
        n_ab = [jnp.where(strict, gi[0:L, 0:PAIR], 0.0) for gi in g]
        a_ak = [jnp.where(strict, gi[0:L, PAIR:], 0.0) for gi in g]
        m_rb = [jnp.where(incl, gi[L:, 0:PAIR], 0.0) for gi in g]
        m_rk = [jnp.where(incl, gi[L:, PAIR:], 0.0) for gi in g]
        n_it = range(len(items))

        lm = level_mask(1)
        xinv = [eye + jnp.where(lm, n_ab[i], 0.0) for i in n_it]
        b = 2
        while b < L:
            lm = level_mask(b)
            t1 = [_dot(xinv[i], _block_diag(jnp.where(lm, n_ab[i], 0.0))) for i in n_it]
            q = [_dot(t1[i], _block_diag(xinv[i])) for i in n_it]
            xinv = [xinv[i] + q[i] for i in n_it]
            b *= 2

        v_bd = [_block_diag(vc[j][:, sls[pr]]) for j, pr in items]
        akv = [_dot(a_ak[i], v_bd[i]) for i in n_it]
        wu = [_dot(xinv[i], jnp.concatenate([_block_diag(a_t[j][:, sls[pr]]), _block_diag(akv[i])], axis=1))
              for i, (j, pr) in enumerate(items)]
        y_loc = [_dot(m_rk[i], v_bd[i]) for i in n_it]

        for j in range(RW_GROUP):
            it = [j * RW_PAIRS + pr for pr in prs]
            s_bd = [st_ref[pr] for pr in prs]
            u_c = [_dot_nt(wu[it[pr]][:, 0:PAIR], s_bd[pr]) + wu[it[pr]][:, PAIR:] for pr in prs]
            y_r = [_dot_nt(r_t[j][:, sls[pr]], s_bd[pr]) for pr in prs]
            y_c = [y_r[pr] + _dot(m_rb[it[pr]], _block_diag(u_c[pr])) + y_loc[it[pr]] for pr in prs]
            upd = [_dot(jnp.concatenate([u_c[pr], vc[j][:, sls[pr]]], axis=0).T,
                        jnp.concatenate([b_h[j][:, sls[pr]], k_h[j][:, sls[pr]]], axis=0)) for pr in prs]
            for pr in prs:
                st_ref[pr] = s_bd[pr] * g_end[j][:, sls[pr]] + jnp.where(bd_mask, upd[pr], 0.0)
                y_s[rows[j], sls[pr]] = y_c[pr]
        return carry

    lax.fori_loop(0, tb // (L * RW_GROUP), group, 0)

    y = y_s[...]
    mean = _seg_sum(y, seg_ones) * (1.0 / HEAD)
    yc = y - mean
    var = _seg_sum(yc * yc, seg_ones) * (1.0 / HEAD)
    yn = yc * lax.rsqrt(var + RW_GN_EPS) * lng_ref[...] + lnb_ref[...]
    o_ref[...] = ((yn + bonus) * gate).astype(o_ref.dtype)


def _rwkv_call(x, ng, w, mu, w0, wl, a0, wg, k_k, k_a, r_k, ln_g, ln_b):
    B, S, _ = x.shape
    tb = RW_TB
    full = lambda shape: pl.BlockSpec(shape, lambda b, i: (0,) * len(shape))
    row = full((1, RW))
    return pl.pallas_call(
        _rwkv_kernel,
        grid=(B, S // tb),
        in_specs=[pl.BlockSpec((None, tb, D_MODEL), lambda b, i: (b, i, 0)),
                  full((1, D_MODEL)), full((D_MODEL, RW_COLS)), full((1, RW_COLS)), row,
                  full((LANES, 2 * RW)), row, full((GATE_LORA, RW)), row, row, row, row, row],
        out_specs=pl.BlockSpec((None, tb, RW), lambda b, i: (b, i, 0)),
        out_shape=jax.ShapeDtypeStruct((B, S, RW), BF16),
        scratch_shapes=[pltpu.VMEM((8, RW_COLS), F32), pltpu.VMEM((RW_PAIRS, PAIR, PAIR), F32)]
        + [pltpu.VMEM((tb, RW), F32)] * 7,
        compiler_params=pltpu.CompilerParams(dimension_semantics=("arbitrary", "arbitrary"),
                                             vmem_limit_bytes=VMEM_LIMIT),
        name="rwkv",
    )(x, ng, w, mu, w0, wl, a0, wg, k_k, k_a, r_k, ln_g, ln_b)


def _ssd_kernel(x_ref, ng_ref, w_ref, cw_ref, cb_ref, dtb_ref, alog_ref, dx_ref, sng_ref, exp_ref, o_ref,
                prev_ref, st_ref, xs_s, xdt_s, b_s, c_s, acs_s, acst_s, y_s):
    tb = x_ref.shape[0]
    L = S_CHUNK

    @pl.when(pl.program_id(1) == 0)
    def _():
        prev_ref[...] = jnp.zeros_like(prev_ref)
        st_ref[...] = jnp.zeros_like(st_ref)

    h = _rms(x_ref[...], ng_ref[...])
    p = jnp.dot(h.astype(BF16), w_ref[...], preferred_element_type=F32)
    z = p[:, 0:SW]
    xbc = p[:, SW:SW + S_XBC]
    dt_raw = p[:, SW + S_XBC:]

    carry = prev_ref[...]
    prev_ref[...] = xbc[tb - 8:tb, :]
    row8 = _iota((8, S_XBC), 0)
    cw = cw_ref[...]
    conv = cb_ref[...] + cw[S_CONV - 1:S_CONV, :] * xbc
    for j in range(1, S_CONV):
        sh = pltpu.roll(xbc, j, 0)
        top = jnp.where(row8 < j, pltpu.roll(carry, j, 0), sh[0:8, :])
        sh = jnp.concatenate([top, sh[8:, :]], axis=0)
        conv = conv + cw[S_CONV - 1 - j:S_CONV - j, :] * sh
    act = conv * _sigmoid(conv)
    xs = act[:, 0:SW]
    b_s[...] = act[:, SW:SW + S_GROUPS * S_STATE]
    c_s[...] = act[:, SW + S_GROUPS * S_STATE:]

    head_lane = _iota((tb, LANES), 1) < S_HEADS
    dt = jnp.where(head_lane, _softplus(dt_raw + dtb_ref[...]), 0.0)
    a_dt = dt * (-jnp.exp(alog_ref[...]))
    expand = exp_ref[...]
    xdt = xs * _dot_sel_r(dt, expand)
    xs_s[...] = xs
    xdt_s[...] = xdt

    ri = _iota((tb, tb), 0)
    ci = _iota((tb, tb), 1)
    same_chunk = (ri // L) == (ci // L)
    tril_blk = (same_chunk & (ci <= ri)).astype(BF16)
    triu_blk = (same_chunk & (ri <= ci)).astype(BF16)
    acs = _dot_sel_l(tril_blk, a_dt)
    acs_s[...] = acs
    acst = _dot_sel_r(a_dt.T, triu_blk)
    for c in range(tb // L):
        acst_s[c] = acst[:, c * L:(c + 1) * L]

    tril_ll = _iota((L, L), 1) <= _iota((L, L), 0)

    def chunk(c, carry_):
        r0 = pl.multiple_of(c * L, L)
        rows = pl.ds(r0, L)
        acs_c = acs_s[rows, :]
        acs_t = acst_s[c]
        a_last = acs_c[L - 1:L, :]
        e_acs = _dot_sel_r(jnp.exp(acs_c), expand)
        e_st = _dot_sel_r(jnp.exp(a_last - acs_c), expand)
        e_end = _dot_sel_r(jnp.broadcast_to(jnp.exp(a_last), (8, LANES)), expand)[0:1, :]
        xdt_c = xdt_s[rows, :]
        xw = xdt_c * e_st
        for g in range(S_GROUPS):
            gs = slice(g * S_STATE, (g + 1) * S_STATE)
            b_g = b_s[rows, gs]
            c_g = c_s[rows, gs]
            cb = _dot_nt(c_g, b_g)
            b_gt = b_g.T
            for q in range(PAIRS_PER_GROUP):
                pr = g * PAIRS_PER_GROUP + q
                sl = slice(pr * PAIR, (pr + 1) * PAIR)
                ms = []
                for hh in (2 * pr, 2 * pr + 1):
                    seg = acs_c[:, hh:hh + 1] - acs_t[hh:hh + 1, :]
                    ms.append(cb * jnp.exp(jnp.where(tril_ll, seg, -jnp.inf)))
                y_d = _dot(jnp.concatenate(ms, axis=1), _block_diag(xdt_c[:, sl]))
                s_p = st_ref[pr]
                y_o = _dot(c_g, s_p) * e_acs[:, sl]
                st_ref[pr] = s_p * e_end[:, sl] + _dot(b_gt, xw[:, sl])
                y_s[rows, sl] = y_d + y_o
        return carry_

    lax.fori_loop(0, tb // L, chunk, 0)

    y = y_s[...] + xs * dx_ref[...]
    uu = y * (z * _sigmoid(z))
    gw = SW // S_GROUPS
    outs = []
    for g in range(S_GROUPS):
        ug = uu[:, g * gw:(g + 1) * gw]
        outs.append(ug * lax.rsqrt(jnp.mean(ug * ug, axis=-1, keepdims=True) + NORM_EPS))
    o_ref[...] = (jnp.concatenate(outs, axis=1) * sng_ref[...]).astype(o_ref.dtype)


def _ssd_call(x, ng, w, cw, cb, dtb, alog, dx, sng, expand):
    B, S, _ = x.shape
    tb = S_TB
    full = lambda shape: pl.BlockSpec(shape, lambda b, i: (0,) * len(shape))
    return pl.pallas_call(
        _ssd_kernel,
        grid=(B, S // tb),
        in_specs=[pl.BlockSpec((None, tb, D_MODEL), lambda b, i: (b, i, 0)),
                  full((1, D_MODEL)), full((D_MODEL, S_COLS_PAD)), full((S_CONV, S_XBC)), full((1, S_XBC)),
                  full((1, LANES)), full((1, LANES)), full((1, SW)), full((1, SW)), full((LANES, SW))],
        out_specs=pl.BlockSpec((None, tb, SW), lambda b, i: (b, i, 0)),
        out_shape=jax.ShapeDtypeStruct((B, S, SW), BF16),
        scratch_shapes=[pltpu.VMEM((8, S_XBC), F32), pltpu.VMEM((S_PAIRS, S_STATE, PAIR), F32),
                        pltpu.VMEM((tb, SW), F32), pltpu.VMEM((tb, SW), F32),
                        pltpu.VMEM((tb, S_GROUPS * S_STATE), F32), pltpu.VMEM((tb, S_GROUPS * S_STATE), F32),
                        pltpu.VMEM((tb, LANES), F32), pltpu.VMEM((tb // S_CHUNK, LANES, S_CHUNK), F32),
                        pltpu.VMEM((tb, SW), F32)],
        compiler_params=pltpu.CompilerParams(dimension_semantics=("arbitrary", "arbitrary"),
                                             vmem_limit_bytes=VMEM_LIMIT),
        name="ssd",
    )(x, ng, w, cw, cb, dtb, alog, dx, sng, expand)


def _merge_kernel(x_ref, ya_ref, yb_ref, ng_ref, wgt_ref, bgt_ref, wa_ref, wb_ref, wo_ref, fg_ref,
                  wrh_ref, wrl_ref, br_ref, x1_ref, h2_ref, rt_ref, cnt_ref, carry_ref, *, tiles_per_moe):
    tm = x_ref.shape[0]
    i = pl.program_id(0)

    @pl.when(i % tiles_per_moe == 0)
    def _():
        carry_ref[...] = jnp.zeros_like(carry_ref)

    x = x_ref[...]
    h = _rms(x, ng_ref[...])
    gates = _sigmoid(jnp.dot(h.astype(BF16), wgt_ref[...], preferred_element_type=F32) + bgt_ref[...])
    up_a = jnp.dot(ya_ref[...], wa_ref[...], preferred_element_type=F32)
    up_b = jnp.dot(yb_ref[...], wb_ref[...], preferred_element_type=F32)
    merged = gates[:, :D_MODEL] * up_a + gates[:, D_MODEL:] * up_b
    x1 = x + _dot(merged, wo_ref[...])
    x1_ref[...] = x1
    h2 = _rms(x1, fg_ref[...])
    h2_ref[...] = h2

    hh, hl = _split(h2)
    wh, wl = wrh_ref[...], wrl_ref[...]
    logits = (jnp.dot(hh, wh, preferred_element_type=F32) + jnp.dot(hh, wl, preferred_element_type=F32)
              + jnp.dot(hl, wh, preferred_element_type=F32)) + br_ref[...]
    li = _iota(logits.shape, 1).astype(F32)
    neg = -jnp.inf
    big = float(LANES)

    gl = jnp.where(li < N_GROUPS, logits, neg)
    gmax = jnp.max(gl, axis=-1, keepdims=True)
    grp = jnp.min(jnp.where(gl == gmax, li, big), axis=-1, keepdims=True)
    g_w = 1.0 / jnp.sum(jnp.exp(gl - gmax), axis=-1, keepdims=True)

    lo_lane = ROUTE_E0 + EPG * grp
    el = jnp.where((li >= lo_lane) & (li < lo_lane + EPG), logits, neg)
    emax = jnp.max(el, axis=-1, keepdims=True)
    i1 = jnp.min(jnp.where(el == emax, li, big), axis=-1, keepdims=True)
    esum = jnp.sum(jnp.exp(el - emax), axis=-1, keepdims=True)
    el2 = jnp.where(li == i1, neg, el)
    m2 = jnp.max(el2, axis=-1, keepdims=True)
    i2 = jnp.min(jnp.where(el2 == m2, li, big), axis=-1, keepdims=True)
    p1 = 1.0 / esum
    p2 = jnp.exp(m2 - emax) / esum
    gate1 = g_w * p1 / (p1 + p2)
    gate2 = g_w * p2 / (p1 + p2)

    hit1 = li == i1
    hit2 = li == i2
    onehot = jnp.where(hit1 | hit2, 1.0, 0.0)
    tril_strict = (_iota((tm, tm), 1) < _iota((tm, tm), 0)).astype(BF16)
    cum = jnp.dot(tril_strict, onehot.astype(BF16), preferred_element_type=F32) + carry_ref[0:1, :]
    r1 = jnp.sum(jnp.where(hit1, cum, 0.0), axis=-1, keepdims=True)
    r2 = jnp.sum(jnp.where(hit2, cum, 0.0), axis=-1, keepdims=True)
    carry_ref[...] = carry_ref[...] + jnp.sum(onehot, axis=0, keepdims=True)
    cnt_ref[...] = carry_ref[...]

    e1 = i1 - ROUTE_E0
    e2 = i2 - ROUTE_E0
    rt = jnp.zeros(logits.shape, F32)
    for lane, val in enumerate((e1, e2, r1, r2, gate1, gate2)):
        rt = jnp.where(li == lane, val, rt)
    rt_ref[...] = rt


def _merge_call(x2, ya, yb, ng, wgt, bgt, wa, wb, wo, fg, wrh, wrl, br, moe_tile):
    T = x2.shape[0]
    tm = MERGE_TM
    tiles_per_moe = moe_tile // tm
    n_moe = T // moe_tile
    full = lambda shape: pl.BlockSpec(shape, lambda i: (0,) * len(shape))
    rows = lambda w: pl.BlockSpec((tm, w), lambda i: (i, 0))
    return pl.pallas_call(
        functools.partial(_merge_kernel, tiles_per_moe=tiles_per_moe),
        grid=(T // tm,),
        in_specs=[rows(D_MODEL), rows(RW), rows(SW), full((1, D_MODEL)), full((D_MODEL, 2 * D_MODEL)),
                  full((1, 2 * D_MODEL)), full((RW, D_MODEL)), full((SW, D_MODEL)), full((D_MODEL, D_MODEL)),
                  full((1, D_MODEL)), full((D_MODEL, LANES)), full((D_MODEL, LANES)), full((1, LANES))],
        out_specs=[rows(D_MODEL), rows(D_MODEL), rows(LANES),
                   pl.BlockSpec((None, 8, LANES), lambda i: (i // tiles_per_moe, 0, 0))],
        out_shape=[jax.ShapeDtypeStruct((T, D_MODEL), F32), jax.ShapeDtypeStruct((T, D_MODEL), F32),
                   jax.ShapeDtypeStruct((T, LANES), F32), jax.ShapeDtypeStruct((n_moe, 8, LANES), F32)],
        scratch_shapes=[pltpu.VMEM((8, LANES), F32)],
        compiler_params=pltpu.CompilerParams(dimension_semantics=("arbitrary",),
                                             vmem_limit_bytes=VMEM_LIMIT),
        name="merge",
    )(x2, ya, yb, ng, wgt, bgt, wa, wb, wo, fg, wrh, wrl, br)


def _moe_kernel(idx_ref, gts_ref, cnt_ref, h2_ref, wg_ref, wu_ref, wd_ref, y_ref,
                start_ref, stok_ref, sg_ref, xb_ref, ob_ref):
    e = pl.program_id(1)
    tt = h2_ref.shape[0]

    @pl.when(e == 0)
    def _():
        def starts(j, acc):
            start_ref[j] = acc
            return acc + cnt_ref[0, j]
        lax.fori_loop(0, N_EXP, starts, jnp.int32(0))

        def place(j, c):
            for u in range(MOE_UNROLL):
                t = j * MOE_UNROLL + u
                for k in range(2):
                    slot = start_ref[idx_ref[0, k * tt + t]] + idx_ref[0, (2 + k) * tt + t]
                    stok_ref[slot] = t
                    sg_ref[slot] = gts_ref[0, k * tt + t]
            return c
        lax.fori_loop(0, tt // MOE_UNROLL, place, 0)
        y_ref[...] = jnp.zeros_like(y_ref)
        xb_ref[...] = jnp.zeros_like(xb_ref)

    n = cnt_ref[0, e]
    s0 = start_ref[e]
    last_slot = 2 * tt - 1

    def block(b, c):
        base = s0 + b * MOE_ROWS
        m = jnp.minimum(n - b * MOE_ROWS, MOE_ROWS)
        groups = (m + MOE_UNROLL - 1) // MOE_UNROLL

        def gather(j, c2):
            dst = xb_ref.at[pl.ds(pl.multiple_of(j * MOE_UNROLL, MOE_UNROLL), MOE_UNROLL)]
            for u in range(MOE_UNROLL):
                t = stok_ref[jnp.minimum(base + j * MOE_UNROLL + u, last_slot)]
                dst[u:u + 1, :] = h2_ref[pl.ds(t, 1), :]
            return c2
        lax.fori_loop(0, groups, gather, 0)

        xb = xb_ref[...].astype(BF16)
        hg = jnp.dot(xb, wg_ref[...], preferred_element_type=F32)
        hu = jnp.dot(xb, wu_ref[...], preferred_element_type=F32)
        hid = hg * _sigmoid(hg) * hu
        ob_ref[...] = jnp.dot(hid.astype(BF16), wd_ref[...], preferred_element_type=F32)

        def scatter(j, c2):
            done = []
            src = ob_ref.at[pl.ds(pl.multiple_of(j * MOE_UNROLL, MOE_UNROLL), MOE_UNROLL)]
            for u in range(MOE_UNROLL):
                slot = base + j * MOE_UNROLL + u
                t = stok_ref[slot]
                done.append((t, y_ref[pl.ds(t, 1), :] + sg_ref[slot] * src[u:u + 1, :]))
            for t, val in done:
                y_ref[pl.ds(t, 1), :] = val
            return c2
        full = m // MOE_UNROLL
        lax.fori_loop(0, full, scatter, 0)

        def scatter_row(r, c2):
            t = stok_ref[base + r]
            y_ref[pl.ds(t, 1), :] = y_ref[pl.ds(t, 1), :] + sg_ref[base + r] * ob_ref[pl.ds(r, 1), :]
            return c2
        lax.fori_loop(full * MOE_UNROLL, m, scatter_row, 0)
        return c

    lax.fori_loop(0, (n + MOE_ROWS - 1) // MOE_ROWS, block, 0)


def _moe_call(idx, gts, cnt, h2, wg, wu, wd, moe_tile):
    T = h2.shape[0]
    tt = moe_tile
    smem = lambda w: pl.BlockSpec((None, 1, w), lambda i, e: (i, 0, 0), memory_space=pltpu.SMEM)
    return pl.pallas_call(
        _moe_kernel,
        grid=(T // tt, N_EXP),
        in_specs=[smem(4 * tt), smem(2 * tt), smem(N_EXP),
                  pl.BlockSpec((tt, D_MODEL), lambda i, e: (i, 0)),
                  pl.BlockSpec((None, D_MODEL, D_EXP), lambda i, e: (e, 0, 0)),
                  pl.BlockSpec((None, D_MODEL, D_EXP), lambda i, e: (e, 0, 0)),
                  pl.BlockSpec((None, D_EXP, D_MODEL), lambda i, e: (e, 0, 0))],
        out_specs=pl.BlockSpec((tt, D_MODEL), lambda i, e: (i, 0)),
        out_shape=jax.ShapeDtypeStruct((T, D_MODEL), F32),
        scratch_shapes=[pltpu.SMEM((N_EXP,), jnp.int32), pltpu.SMEM((2 * tt,), jnp.int32),
                        pltpu.SMEM((2 * tt,), F32), pltpu.VMEM((MOE_ROWS, D_MODEL), F32),
                        pltpu.VMEM((MOE_ROWS, D_MODEL), F32)],
        compiler_params=pltpu.CompilerParams(dimension_semantics=("arbitrary", "arbitrary"),
                                             vmem_limit_bytes=VMEM_LIMIT),
        name="moe",
    )(idx, gts, cnt, h2, wg, wu, wd)


def _final_kernel(x1_ref, y_ref, g_ref, o_ref):
    o_ref[...] = _rms(x1_ref[...] + y_ref[...], g_ref[...])


def _final_call(x1, y, g):
    T = x1.shape[0]
    tm = 512
    rows = pl.BlockSpec((tm, D_MODEL), lambda i: (i, 0))
    return pl.pallas_call(
        _final_kernel,
        grid=(T // tm,),
        in_specs=[rows, rows, pl.BlockSpec((1, D_MODEL), lambda i: (0, 0))],
        out_specs=rows,
        out_shape=jax.ShapeDtypeStruct((T, D_MODEL), F32),
        compiler_params=pltpu.CompilerParams(dimension_semantics=("arbitrary",)),
        name="final",
    )(x1, y, g)


def _layer(x, attn_norm_g, w_in, b_gate, rwkv_mu, rwkv_w0, rwkv_w_decay, rwkv_a0, rwkv_w_a, rwkv_w_g,
           rwkv_k_k, rwkv_k_a, rwkv_r_k, rwkv_ln_g, rwkv_ln_b, w_up_rwkv, ssd_conv_w, ssd_conv_b,
           ssd_dt_bias, ssd_a_log, ssd_d, ssd_norm_g, w_up_ssd, w_out, ffn_norm_g, w_router_group,
           b_router_group, w_router_expert, b_router_expert, w_exp_gate, w_exp_up, w_exp_down):
    B, S, _ = x.shape
    T = B * S
    row = lambda a: a.reshape(1, -1).astype(F32)
    ng = row(attn_norm_g)

    w_rw = w_in[:, :RW_COLS].astype(BF16)
    wl = jnp.zeros((LANES, 2 * RW), F32)
    wl = wl.at[:DECAY_LORA, :RW].set(rwkv_w_decay).at[DECAY_LORA:, RW:].set(rwkv_w_a).astype(BF16)
    y_a = _rwkv_call(x, ng, w_rw, row(rwkv_mu), row(rwkv_w0), wl, row(rwkv_a0), rwkv_w_g.astype(BF16),
                     row(rwkv_k_k), row(rwkv_k_a), row(rwkv_r_k), row(rwkv_ln_g), row(rwkv_ln_b))

    n_ssd = 2 * SW + 2 * S_GROUPS * S_STATE + S_HEADS
    w_ssd = jnp.pad(w_in[:, RW_COLS:RW_COLS + n_ssd], ((0, 0), (0, S_COLS_PAD - n_ssd))).astype(BF16)
    pad_h = lambda a: jnp.pad(a.reshape(1, -1).astype(F32), ((0, 0), (0, LANES - S_HEADS)))
    expand = (jnp.arange(LANES)[:, None] == (jnp.arange(SW) // HEAD)[None, :]).astype(BF16)
    y_b = _ssd_call(x, ng, w_ssd, ssd_conv_w.astype(F32), row(ssd_conv_b), pad_h(ssd_dt_bias), pad_h(ssd_a_log),
                    row(jnp.repeat(ssd_d, HEAD)), row(ssd_norm_g), expand)

    moe_tile = min(MOE_TILE, T)
    w_gt = w_in[:, RW_COLS + n_ssd:].astype(BF16)
    w_r = jnp.zeros((D_MODEL, LANES), F32)
    w_r = w_r.at[:, :N_GROUPS].set(w_router_group).at[:, ROUTE_E0:ROUTE_E0 + N_EXP].set(w_router_expert)
    w_rh = w_r.astype(BF16)
    w_rl = (w_r - w_rh.astype(F32)).astype(BF16)
    b_r = jnp.zeros((1, LANES), F32)
    b_r = b_r.at[0, :N_GROUPS].set(b_router_group).at[0, ROUTE_E0:ROUTE_E0 + N_EXP].set(b_router_expert)
    x1, h2, route, cnt = _merge_call(
        x.reshape(T, D_MODEL), y_a.reshape(T, RW), y_b.reshape(T, SW), ng, w_gt, row(b_gate),
        w_up_rwkv.astype(BF16), w_up_ssd.astype(BF16), w_out.astype(BF16), row(ffn_norm_g), w_rh, w_rl, b_r,
        moe_tile)

    n_moe = T // moe_tile
    idx = route[:, 0:4].astype(jnp.int32).reshape(n_moe, moe_tile, 4).transpose(0, 2, 1).reshape(n_moe, 1, 4 * moe_tile)
    gts = route[:, 4:6].reshape(n_moe, moe_tile, 2).transpose(0, 2, 1).reshape(n_moe, 1, 2 * moe_tile)
    counts = cnt[:, 0:1, ROUTE_E0:ROUTE_E0 + N_EXP].astype(jnp.int32)
    y_moe = _moe_call(idx, gts, counts, h2, w_exp_gate.astype(BF16), w_exp_up.astype(BF16),
                      w_exp_down.astype(BF16), moe_tile)
    return x1, y_moe


def kernel(x, attn_norm_g, w_in, b_gate, rwkv_mu, rwkv_w0, rwkv_w_decay, rwkv_a0, rwkv_w_a, rwkv_w_g, rwkv_k_k, rwkv_k_a, rwkv_r_k, rwkv_ln_g, rwkv_ln_b, w_up_rwkv, ssd_conv_w, ssd_conv_b, ssd_dt_bias, ssd_a_log, ssd_d, ssd_norm_g, w_up_ssd, w_out, ffn_norm_g, w_router_group, b_router_group, w_router_expert, b_router_expert, w_exp_gate, w_exp_up, w_exp_down, final_norm_g):
    B, S, _ = x.shape
    depth = attn_norm_g.shape[0]
    assert depth == 1, "the final residual add is fused with the last layer's MoE output"
    layer_params = (attn_norm_g, w_in, b_gate, rwkv_mu, rwkv_w0, rwkv_w_decay, rwkv_a0, rwkv_w_a, rwkv_w_g,
                    rwkv_k_k, rwkv_k_a, rwkv_r_k, rwkv_ln_g, rwkv_ln_b, w_up_rwkv, ssd_conv_w, ssd_conv_b,
                    ssd_dt_bias, ssd_a_log, ssd_d, ssd_norm_g, w_up_ssd, w_out, ffn_norm_g, w_router_group,
                    b_router_group, w_router_expert, b_router_expert, w_exp_gate, w_exp_up, w_exp_down)
    x1, y_moe = _layer(x, *(prm[0] for prm in layer_params))
    out = _final_call(x1, y_moe, final_norm_g.reshape(1, -1).astype(F32))
    return out.reshape(B, S, D_MODEL)
```

```python
import functools

import jax
import jax.numpy as jnp
from jax import lax
from jax.experimental import pallas as pl
from jax.experimental.pallas import tpu as pltpu

F32 = jnp.float32
BF16 = jnp.bfloat16

D_MODEL = 1024
NORM_EPS = 1e-6
LANES = 128
HEAD = 64
PAIR = 2 * HEAD

RW = 1024
RW_SLAB = PAIR
RW_SLABS = RW // RW_SLAB
DECAY_LORA = 64
AAA_LORA = 64
GATE_LORA = 128
RW_COLS = 3 * RW + DECAY_LORA + AAA_LORA + GATE_LORA
RW_GN_EPS = 64e-5
RW_CHUNK = 64
RW_GROUP = 4
RW_TB = 256
EXP_NEG_HALF = 0.6065306597126334

SW = 2048
S_HEADS = SW // HEAD
S_PAIRS = SW // PAIR
S_GROUPS = 4
S_STATE = 128
S_CONV = 4
S_XBC = SW + 2 * S_GROUPS * S_STATE
S_COLS_PAD = SW + S_XBC + LANES
S_CHUNK = 128
S_TB = 256
PAIRS_PER_GROUP = S_PAIRS // S_GROUPS

N_GROUPS = 4
EPG = 8
N_EXP = 32
D_EXP = 512
MOE_ROWS = 128
MOE_TILE = 4096
MOE_UNROLL = 8
MERGE_TM = 512
ROUTE_E0 = N_GROUPS

VMEM_LIMIT = 56 * 1024 * 1024


def _dot(a, b):
    return jnp.dot(a.astype(BF16), b.astype(BF16), preferred_element_type=F32)


def _dot_nt(a, b):
    return lax.dot_general(a.astype(BF16), b.astype(BF16), (((1,), (1,)), ((), ())),
                           preferred_element_type=F32)


def _split(x):
    hi = x.astype(BF16)
    lo = (x - hi.astype(F32)).astype(BF16)
    return hi, lo


def _dot_sel_l(sel, x):
    hi, lo = _split(x)
    return (jnp.dot(sel, hi, preferred_element_type=F32)
            + jnp.dot(sel, lo, preferred_element_type=F32))


def _dot_sel_r(x, sel):
    hi, lo = _split(x)
    return (jnp.dot(hi, sel, preferred_element_type=F32)
            + jnp.dot(lo, sel, preferred_element_type=F32))


def _iota(shape, dim):
    return lax.broadcasted_iota(jnp.int32, shape, dim)


def _softplus(x):
    return jnp.maximum(x, 0.0) + jnp.log(1.0 + jnp.exp(-jnp.abs(x)))


def _sigmoid(x):
    return 1.0 / (1.0 + jnp.exp(-x))


def _rms(x, g):
    return x * lax.rsqrt(jnp.mean(x * x, axis=-1, keepdims=True) + NORM_EPS) * g


def _block_diag(y):
    head = _iota(y.shape, 1) // HEAD
    yb = y.astype(BF16)
    zero = jnp.zeros_like(yb)
    return jnp.concatenate([jnp.where(head == i, yb, zero) for i in range(y.shape[1] // HEAD)], axis=0)


def _seg_sum(x, seg_ones, split=True):
    outs = []
    w = seg_ones.shape[0]
    for c in range(x.shape[1] // w):
        xc = x[:, c * w:(c + 1) * w]
        outs.append(_dot_sel_r(xc, seg_ones) if split else _dot(xc, seg_ones))
    return jnp.concatenate(outs, axis=1)


def _rwkv_kernel(x_ref, ng_ref, w_ref, mu_ref, w0_ref, wl_ref, a0_ref, wg_ref, kk_ref, ka_ref, rk_ref,
                 lng_ref, lnb_ref, o_ref,
                 prev_ref, st_ref, r_s, k_s, kk_s, b_s, v_s, ld_s, y_s):
    tb = x_ref.shape[0]
    L = RW_CHUNK

    @pl.when(pl.program_id(1) == 0)
    def _():
        prev_ref[...] = jnp.zeros_like(prev_ref)
        st_ref[...] = jnp.zeros_like(st_ref)

    h = _rms(x_ref[...], ng_ref[...])
    p = jnp.dot(h.astype(BF16), w_ref[...], preferred_element_type=F32)

    rolled = pltpu.roll(p, 1, 0)
    top = jnp.where(_iota((8, RW_COLS), 0) == 0, prev_ref[7:8, :], rolled[0:8, :])
    shifted = jnp.concatenate([top, rolled[8:, :]], axis=0)
    prev_ref[...] = p[tb - 8:tb, :]
    u = p + mu_ref[...] * (shifted - p)

    r = u[:, 0:RW]
    k = u[:, RW:2 * RW]
    v = u[:, 2 * RW:3 * RW]
    zwa = u[:, 3 * RW:3 * RW + LANES]
    zg = u[:, 3 * RW + LANES:3 * RW + 2 * LANES]
    lora_in = jnp.where(_iota(zwa.shape, 1) < DECAY_LORA, jnp.tanh(zwa), zwa)
    dl = _dot(lora_in, wl_ref[...])
    ld = -EXP_NEG_HALF * _sigmoid(w0_ref[...] + dl[:, :RW])
    a = _sigmoid(a0_ref[...] + dl[:, RW:])
    gate = _dot(_sigmoid(zg), wg_ref[...])

    seg_ones = (_iota((2 * PAIR, 2 * PAIR), 0) // HEAD == _iota((2 * PAIR, 2 * PAIR), 1) // HEAD).astype(BF16)
    kk = k * kk_ref[...]
    kk = kk * lax.rsqrt(jnp.maximum(_seg_sum(kk * kk, seg_ones, split=False), 1e-24))
    k2 = k * (1.0 + (a - 1.0) * ka_ref[...])
    bonus = _seg_sum(r * k2 * rk_ref[...], seg_ones) * v

    r_s[...] = r
    k_s[...] = k2
    kk_s[...] = kk
    b_s[...] = kk * a
    v_s[...] = v
    ld_s[...] = ld

    W = RW_SLAB
    t_i = _iota((L, W), 0)
    s_i = _iota((L, W), 1) % HEAD
    strict = s_i < t_i
    incl = s_i <= t_i
    eye = (s_i == t_i).astype(F32)
    tril_l = (_iota((L, L), 1) <= _iota((L, L), 0)).astype(BF16)
    bd_mask = (_iota((W, W), 0) // HEAD) == (_iota((W, W), 1) // HEAD)

    def level_mask(b):
        same = (t_i // (2 * b)) == (s_i // (2 * b))
        return same & ((t_i // b) % 2 == 1) & ((s_i // b) % 2 == 0)

    prs = range(RW_SLABS)
    sls = [slice(pr * W, (pr + 1) * W) for pr in prs]

    def group(cg, carry):
        rows, a_t, r_t, b_t, k_t, b_h, k_h, vc, g_end = [], [], [], [], [], [], [], [], []
        for j in range(RW_GROUP):
            rj = pl.ds(pl.multiple_of((cg * RW_GROUP + j) * L, L), L)
            ldc = ld_s[rj, :]
            cs = _dot_sel_l(tril_l, ldc)
            cs_last = cs[L - 1:L, :]
            e_inv = jnp.exp(-cs)
            e_end = jnp.exp(cs_last - cs)
            bc = b_s[rj, :]
            kc = k_s[rj, :]
            rows.append(rj)
            a_t.append(-kk_s[rj, :] * jnp.exp(cs - ldc))
            r_t.append(r_s[rj, :] * jnp.exp(cs))
            b_t.append(bc * e_inv)
            k_t.append(kc * e_inv)
            b_h.append(bc * e_end)
            k_h.append(kc * e_end)
            vc.append(v_s[rj, :])
            g_end.append(jnp.exp(cs_last))

        items = [(j, pr) for j in range(RW_GROUP) for pr in prs]
        g = [_dot_nt(jnp.concatenate([a_t[j][:, sls[pr]], r_t[j][:, sls[pr]]], axis=0),
                     jnp.concatenate([_block_diag(b_t[j][:, sls[pr]]), _block_diag(k_t[j][:, sls[pr]])], axis=0))
             for j, pr in items]
        n_ab = [jnp.where(strict, gi[0:L, 0:W], 0.0) for gi in g]
        a_ak = [jnp.where(strict, gi[0:L, W:], 0.0) for gi in g]
        m_rb = [jnp.where(incl, gi[L:, 0:W], 0.0) for gi in g]
        m_rk = [jnp.where(incl, gi[L:, W:], 0.0) for gi in g]
        n_it = range(len(items))

        lm = level_mask(1)
        xinv = [eye + jnp.where(lm, n_ab[i], 0.0) for i in n_it]
        b = 2
        while b < L:
            lm = level_mask(b)
            t1 = [_dot(xinv[i], _block_diag(jnp.where(lm, n_ab[i], 0.0))) for i in n_it]
            q = [_dot(t1[i], _block_diag(xinv[i])) for i in n_it]
            xinv = [xinv[i] + q[i] for i in n_it]
            b *= 2

        v_bd = [_block_diag(vc[j][:, sls[pr]]) for j, pr in items]
        akv = [_dot(a_ak[i], v_bd[i]) for i in n_it]
        wu = [_dot(xinv[i], jnp.concatenate([_block_diag(a_t[j][:, sls[pr]]), _block_diag(akv[i])], axis=1))
              for i, (j, pr) in enumerate(items)]
        y_loc = [_dot(m_rk[i], v_bd[i]) for i in n_it]

        for j in range(RW_GROUP):
            it = [j * RW_SLABS + pr for pr in prs]
            s_bd = [st_ref[pr] for pr in prs]
            u_c = [_dot_nt(wu[it[pr]][:, 0:W], s_bd[pr]) + wu[it[pr]][:, W:] for pr in prs]
            y_r = [_dot_nt(r_t[j][:, sls[pr]], s_bd[pr]) for pr in prs]
            y_c = [y_r[pr] + _dot(m_rb[it[pr]], _block_diag(u_c[pr])) + y_loc[it[pr]] for pr in prs]
            upd = [_dot(jnp.concatenate([u_c[pr], vc[j][:, sls[pr]]], axis=0).T,
                        jnp.concatenate([b_h[j][:, sls[pr]], k_h[j][:, sls[pr]]], axis=0)) for pr in prs]
            for pr in prs:
                st_ref[pr] = s_bd[pr] * g_end[j][:, sls[pr]] + jnp.where(bd_mask, upd[pr], 0.0)
                y_s[rows[j], sls[pr]] = y_c[pr]
        return carry

    lax.fori_loop(0, tb // (L * RW_GROUP), group, 0)

    y = y_s[...]
    mean = _seg_sum(y, seg_ones) * (1.0 / HEAD)
    yc = y - mean
    var = _seg_sum(yc * yc, seg_ones, split=False) * (1.0 / HEAD)
    yn = yc * lax.rsqrt(var + RW_GN_EPS) * lng_ref[...] + lnb_ref[...]
    o_ref[...] = ((yn + bonus) * gate).astype(o_ref.dtype)


def _rwkv_call(x, ng, w, mu, w0, wl, a0, wg, k_k, k_a, r_k, ln_g, ln_b):
    B, S, _ = x.shape
    tb = RW_TB
    full = lambda shape: pl.BlockSpec(shape, lambda b, i: (0,) * len(shape))
    row = full((1, RW))
    return pl.pallas_call(
        _rwkv_kernel,
        grid=(B, S // tb),
        in_specs=[pl.BlockSpec((None, tb, D_MODEL), lambda b, i: (b, i, 0)),
                  full((1, D_MODEL)), full((D_MODEL, RW_COLS)), full((1, RW_COLS)), row,
                  full((LANES, 2 * RW)), row, full((GATE_LORA, RW)), row, row, row, row, row],
        out_specs=pl.BlockSpec((None, tb, RW), lambda b, i: (b, i, 0)),
        out_shape=jax.ShapeDtypeStruct((B, S, RW), BF16),
        scratch_shapes=[pltpu.VMEM((8, RW_COLS), F32), pltpu.VMEM((RW_SLABS, RW_SLAB, RW_SLAB), F32)]
        + [pltpu.VMEM((tb, RW), F32)] * 7,
        compiler_params=pltpu.CompilerParams(dimension_semantics=("arbitrary", "arbitrary"),
                                             vmem_limit_bytes=VMEM_LIMIT),
        name="rwkv",
    )(x, ng, w, mu, w0, wl, a0, wg, k_k, k_a, r_k, ln_g, ln_b)


def _ssd_kernel(x_ref, ng_ref, w_ref, cw_ref, cb_ref, dtb_ref, alog_ref, dx_ref, sng_ref, exp_ref, o_ref,
                prev_ref, st_ref, xs_s, b_s, c_s, acs_s, acst_s, dtt_s, y_s):
    tb = x_ref.shape[0]
    L = S_CHUNK

    @pl.when(pl.program_id(1) == 0)
    def _():
        prev_ref[...] = jnp.zeros_like(prev_ref)
        st_ref[...] = jnp.zeros_like(st_ref)

    h = _rms(x_ref[...], ng_ref[...])
    p = jnp.dot(h.astype(BF16), w_ref[...], preferred_element_type=F32)
    z = p[:, 0:SW]
    xbc = p[:, SW:SW + S_XBC]
    dt_raw = p[:, SW + S_XBC:]

    carry = prev_ref[...]
    prev_ref[...] = xbc[tb - 8:tb, :]
    row8 = _iota((8, S_XBC), 0)
    conv = cb_ref[...] + cw_ref[S_CONV - 1:S_CONV, :] * xbc
    for j in range(1, S_CONV):
        sh = pltpu.roll(xbc, j, 0)
        top = jnp.where(row8 < j, pltpu.roll(carry, j, 0), sh[0:8, :])
        sh = jnp.concatenate([top, sh[8:, :]], axis=0)
        conv = conv + cw_ref[S_CONV - 1 - j:S_CONV - j, :] * sh
    act = conv * _sigmoid(conv)
    xs = act[:, 0:SW]
    b_s[...] = act[:, SW:SW + S_GROUPS * S_STATE]
    c_s[...] = act[:, SW + S_GROUPS * S_STATE:]

    head_lane = _iota((tb, LANES), 1) < S_HEADS
    dt = jnp.where(head_lane, _softplus(dt_raw + dtb_ref[...]), 0.0)
    a_dt = dt * (-jnp.exp(alog_ref[...]))
    expand = exp_ref[...]
    xs_s[...] = xs

    ri = _iota((tb, tb), 0)
    ci = _iota((tb, tb), 1)
    same_chunk = (ri // L) == (ci // L)
    tril_blk = (same_chunk & (ci <= ri)).astype(BF16)
    triu_blk = (same_chunk & (ri <= ci)).astype(BF16)
    acs = _dot_sel_l(tril_blk, a_dt)
    acs_s[...] = acs
    acst = _dot_sel_r(a_dt.T, triu_blk)
    dtt = dt.T
    for c in range(tb // L):
        acst_s[c] = acst[:, c * L:(c + 1) * L]
        dtt_s[c] = dtt[:, c * L:(c + 1) * L]

    tril_ll = _iota((L, L), 1) <= _iota((L, L), 0)

    def chunk(c, carry_):
        r0 = pl.multiple_of(c * L, L)
        rows = pl.ds(r0, L)
        acs_c = acs_s[rows, :]
        acs_t = acst_s[c]
        dt_t = dtt_s[c]
        a_last = acs_c[L - 1:L, :]
        e_end = _dot_sel_r(jnp.broadcast_to(jnp.exp(a_last), (8, LANES)), expand)[0:1, :]
        xs_c = xs_s[rows, :]
        for g in range(S_GROUPS):
            gs = slice(g * S_STATE, (g + 1) * S_STATE)
            b_g = b_s[rows, gs]
            c_g = c_s[rows, gs]
            cb = _dot_nt(c_g, b_g)
            b_gt = b_g.T
            for q in range(PAIRS_PER_GROUP):
                pr = g * PAIRS_PER_GROUP + q
                sl = slice(pr * PAIR, (pr + 1) * PAIR)
                ms, cs_, bw = [], [], []
                for hh in (2 * pr, 2 * pr + 1):
                    col = jnp.broadcast_to(acs_c[:, hh:hh + 1], (L, L))
                    row = acs_t[hh:hh + 1, :] - jnp.log(dt_t[hh:hh + 1, :])
                    ms.append(cb * jnp.exp(jnp.where(tril_ll, col - row, -jnp.inf)))
                    cs_.append(c_g * jnp.exp(col))
                    bw.append(b_gt * jnp.exp(acs_t[hh:hh + 1, L - 1:L] - row))
                xs_bd = _block_diag(xs_c[:, sl])
                s_p = st_ref[pr]
                y_s[rows, sl] = _dot(jnp.concatenate(ms + cs_, axis=1),
                                     jnp.concatenate([xs_bd, _block_diag(s_p)], axis=0))
                st_ref[pr] = s_p * e_end[:, sl] + _dot(jnp.concatenate(bw, axis=1), xs_bd)
        return carry_

    lax.fori_loop(0, tb // L, chunk, 0)

    y = y_s[...] + xs * dx_ref[...]
    uu = y * (z * _sigmoid(z))
    gw = SW // S_GROUPS
    outs = []
    for g in range(S_GROUPS):
        ug = uu[:, g * gw:(g + 1) * gw]
        outs.append(ug * lax.rsqrt(jnp.mean(ug * ug, axis=-1, keepdims=True) + NORM_EPS))
    o_ref[...] = (jnp.concatenate(outs, axis=1) * sng_ref[...]).astype(o_ref.dtype)


def _ssd_call(x, ng, w, cw, cb, dtb, alog, dx, sng, expand):
    B, S, _ = x.shape
    tb = S_TB
    full = lambda shape: pl.BlockSpec(shape, lambda b, i: (0,) * len(shape))
    return pl.pallas_call(
        _ssd_kernel,
        grid=(B, S // tb),
        in_specs=[pl.BlockSpec((None, tb, D_MODEL), lambda b, i: (b, i, 0)),
                  full((1, D_MODEL)), full((D_MODEL, S_COLS_PAD)), full((S_CONV, S_XBC)), full((1, S_XBC)),
                  full((1, LANES)), full((1, LANES)), full((1, SW)), full((1, SW)), full((LANES, SW))],
        out_specs=pl.BlockSpec((None, tb, SW), lambda b, i: (b, i, 0)),
        out_shape=jax.ShapeDtypeStruct((B, S, SW), BF16),
        scratch_shapes=[pltpu.VMEM((8, S_XBC), F32), pltpu.VMEM((S_PAIRS, S_STATE, PAIR), F32),
                        pltpu.VMEM((tb, SW), F32),
                        pltpu.VMEM((tb, S_GROUPS * S_STATE), F32), pltpu.VMEM((tb, S_GROUPS * S_STATE), F32),
                        pltpu.VMEM((tb, LANES), F32), pltpu.VMEM((tb // S_CHUNK, LANES, S_CHUNK), F32),
                        pltpu.VMEM((tb // S_CHUNK, LANES, S_CHUNK), F32), pltpu.VMEM((tb, SW), F32)],
        compiler_params=pltpu.CompilerParams(dimension_semantics=("arbitrary", "arbitrary"),
                                             vmem_limit_bytes=VMEM_LIMIT),
        name="ssd",
    )(x, ng, w, cw, cb, dtb, alog, dx, sng, expand)


def _merge_kernel(x_ref, ya_ref, yb_ref, ng_ref, wgt_ref, bgt_ref, wa_ref, wb_ref, wo_ref, fg_ref,
                  wrh_ref, wrl_ref, br_ref, x1_ref, h2_ref, rt_ref, cnt_ref, carry_ref, *, tiles_per_moe):
    tm = x_ref.shape[0]
    i = pl.program_id(0)

    @pl.when(i % tiles_per_moe == 0)
    def _():
        carry_ref[...] = jnp.zeros_like(carry_ref)

    x = x_ref[...]
    h = _rms(x, ng_ref[...])
    gates = _sigmoid(jnp.dot(h.astype(BF16), wgt_ref[...], preferred_element_type=F32) + bgt_ref[...])
    up_a = jnp.dot(ya_ref[...], wa_ref[...], preferred_element_type=F32)
    up_b = jnp.dot(yb_ref[...], wb_ref[...], preferred_element_type=F32)
    merged = gates[:, :D_MODEL] * up_a + gates[:, D_MODEL:] * up_b
    x1 = x + _dot(merged, wo_ref[...])
    x1_ref[...] = x1
    h2 = _rms(x1, fg_ref[...])
    h2_ref[...] = h2

    hh, hl = _split(h2)
    wh, wl = wrh_ref[...], wrl_ref[...]
    logits = (jnp.dot(hh, wh, preferred_element_type=F32) + jnp.dot(hh, wl, preferred_element_type=F32)
              + jnp.dot(hl, wh, preferred_element_type=F32)) + br_ref[...]
    li = _iota(logits.shape, 1).astype(F32)
    neg = -jnp.inf
    big = float(LANES)

    gl = jnp.where(li < N_GROUPS, logits, neg)
    gmax = jnp.max(gl, axis=-1, keepdims=True)
    grp = jnp.min(jnp.where(gl == gmax, li, big), axis=-1, keepdims=True)
    g_w = 1.0 / jnp.sum(jnp.exp(gl - gmax), axis=-1, keepdims=True)

    lo_lane = ROUTE_E0 + EPG * grp
    el = jnp.where((li >= lo_lane) & (li < lo_lane + EPG), logits, neg)
    emax = jnp.max(el, axis=-1, keepdims=True)
    i1 = jnp.min(jnp.where(el == emax, li, big), axis=-1, keepdims=True)
    esum = jnp.sum(jnp.exp(el - emax), axis=-1, keepdims=True)
    el2 = jnp.where(li == i1, neg, el)
    m2 = jnp.max(el2, axis=-1, keepdims=True)
    i2 = jnp.min(jnp.where(el2 == m2, li, big), axis=-1, keepdims=True)
    p1 = 1.0 / esum
    p2 = jnp.exp(m2 - emax) / esum
    gate1 = g_w * p1 / (p1 + p2)
    gate2 = g_w * p2 / (p1 + p2)

    hit1 = li == i1
    hit2 = li == i2
    onehot = jnp.where(hit1 | hit2, 1.0, 0.0)
    tril_strict = (_iota((tm, tm), 1) < _iota((tm, tm), 0)).astype(BF16)
    cum = jnp.dot(tril_strict, onehot.astype(BF16), preferred_element_type=F32) + carry_ref[0:1, :]
    r1 = jnp.sum(jnp.where(hit1, cum, 0.0), axis=-1, keepdims=True)
    r2 = jnp.sum(jnp.where(hit2, cum, 0.0), axis=-1, keepdims=True)
    carry_ref[...] = carry_ref[...] + jnp.sum(onehot, axis=0, keepdims=True)
    cnt_ref[...] = carry_ref[...]

    e1 = i1 - ROUTE_E0
    e2 = i2 - ROUTE_E0
    rt = jnp.zeros(logits.shape, F32)
    for lane, val in enumerate((e1, e2, r1, r2, gate1, gate2)):
        rt = jnp.where(li == lane, val, rt)
    rt_ref[...] = rt


def _merge_call(x2, ya, yb, ng, wgt, bgt, wa, wb, wo, fg, wrh, wrl, br, moe_tile):
    T = x2.shape[0]
    tm = MERGE_TM
    tiles_per_moe = moe_tile // tm
    n_moe = T // moe_tile
    full = lambda shape: pl.BlockSpec(shape, lambda i: (0,) * len(shape))
    rows = lambda w: pl.BlockSpec((tm, w), lambda i: (i, 0))
    return pl.pallas_call(
        functools.partial(_merge_kernel, tiles_per_moe=tiles_per_moe),
        grid=(T // tm,),
        in_specs=[rows(D_MODEL), rows(RW), rows(SW), full((1, D_MODEL)), full((D_MODEL, 2 * D_MODEL)),
                  full((1, 2 * D_MODEL)), full((RW, D_MODEL)), full((SW, D_MODEL)), full((D_MODEL, D_MODEL)),
                  full((1, D_MODEL)), full((D_MODEL, LANES)), full((D_MODEL, LANES)), full((1, LANES))],
        out_specs=[rows(D_MODEL), rows(D_MODEL), rows(LANES),
                   pl.BlockSpec((None, 8, LANES), lambda i: (i // tiles_per_moe, 0, 0))],
        out_shape=[jax.ShapeDtypeStruct((T, D_MODEL), F32), jax.ShapeDtypeStruct((T, D_MODEL), F32),
                   jax.ShapeDtypeStruct((T, LANES), F32), jax.ShapeDtypeStruct((n_moe, 8, LANES), F32)],
        scratch_shapes=[pltpu.VMEM((8, LANES), F32)],
        compiler_params=pltpu.CompilerParams(dimension_semantics=("arbitrary",),
                                             vmem_limit_bytes=VMEM_LIMIT),
        name="merge",
    )(x2, ya, yb, ng, wgt, bgt, wa, wb, wo, fg, wrh, wrl, br)


def _moe_kernel(idx_ref, gts_ref, cnt_ref, h2_ref, wg_ref, wu_ref, wd_ref, y_ref,
                start_ref, stok_ref, sg_ref, xb_ref, ob_ref):
    e = pl.program_id(1)
    tt = h2_ref.shape[0]

    @pl.when(e == 0)
    def _():
        def starts(j, acc):
            start_ref[j] = acc
            return acc + cnt_ref[0, j]
        lax.fori_loop(0, N_EXP, starts, jnp.int32(0))

        def place(j, c):
            for u in range(MOE_UNROLL):
                t = j * MOE_UNROLL + u
                for k in range(2):
                    slot = start_ref[idx_ref[0, k * tt + t]] + idx_ref[0, (2 + k) * tt + t]
                    stok_ref[slot] = t
                    sg_ref[slot] = gts_ref[0, k * tt + t]
            return c
        lax.fori_loop(0, tt // MOE_UNROLL, place, 0)
        y_ref[...] = jnp.zeros_like(y_ref)
        xb_ref[...] = jnp.zeros_like(xb_ref)

    n = cnt_ref[0, e]
    s0 = start_ref[e]
    last_slot = 2 * tt - 1

    def block(b, c):
        base = s0 + b * MOE_ROWS
        m = jnp.minimum(n - b * MOE_ROWS, MOE_ROWS)
        groups = (m + MOE_UNROLL - 1) // MOE_UNROLL

        def gather(j, c2):
            dst = xb_ref.at[pl.ds(pl.multiple_of(j * MOE_UNROLL, MOE_UNROLL), MOE_UNROLL)]
            for u in range(MOE_UNROLL):
                t = stok_ref[jnp.minimum(base + j * MOE_UNROLL + u, last_slot)]
                dst[u:u + 1, :] = h2_ref[pl.ds(t, 1), :]
            return c2
        lax.fori_loop(0, groups, gather, 0)

        xb = xb_ref[...].astype(BF16)
        hg = jnp.dot(xb, wg_ref[...], preferred_element_type=F32)
        hu = jnp.dot(xb, wu_ref[...], preferred_element_type=F32)
        hid = hg * _sigmoid(hg) * hu
        ob_ref[...] = jnp.dot(hid.astype(BF16), wd_ref[...], preferred_element_type=F32)

        def scatter(j, c2):
            done = []
            src = ob_ref.at[pl.ds(pl.multiple_of(j * MOE_UNROLL, MOE_UNROLL), MOE_UNROLL)]
            for u in range(MOE_UNROLL):
                slot = base + j * MOE_UNROLL + u
                t = stok_ref[slot]
                done.append((t, y_ref[pl.ds(t, 1), :] + sg_ref[slot] * src[u:u + 1, :]))
            for t, val in done:
                y_ref[pl.ds(t, 1), :] = val
            return c2
        full = m // MOE_UNROLL
        lax.fori_loop(0, full, scatter, 0)

        def scatter_row(r, c2):
            t = stok_ref[base + r]
            y_ref[pl.ds(t, 1), :] = y_ref[pl.ds(t, 1), :] + sg_ref[base + r] * ob_ref[pl.ds(r, 1), :]
            return c2
        lax.fori_loop(full * MOE_UNROLL, m, scatter_row, 0)
        return c

    lax.fori_loop(0, (n + MOE_ROWS - 1) // MOE_ROWS, block, 0)


def _moe_call(idx, gts, cnt, h2, wg, wu, wd, moe_tile):
    T = h2.shape[0]
    tt = moe_tile
    smem = lambda w: pl.BlockSpec((None, 1, w), lambda i, e: (i, 0, 0), memory_space=pltpu.SMEM)
    return pl.pallas_call(
        _moe_kernel,
        grid=(T // tt, N_EXP),
        in_specs=[smem(4 * tt), smem(2 * tt), smem(N_EXP),
                  pl.BlockSpec((tt, D_MODEL), lambda i, e: (i, 0), pipeline_mode=pl.Buffered(1)),
                  pl.BlockSpec((None, D_MODEL, D_EXP), lambda i, e: (e, 0, 0)),
                  pl.BlockSpec((None, D_MODEL, D_EXP), lambda i, e: (e, 0, 0)),
                  pl.BlockSpec((None, D_EXP, D_MODEL), lambda i, e: (e, 0, 0))],
        out_specs=pl.BlockSpec((tt, D_MODEL), lambda i, e: (i, 0), pipeline_mode=pl.Buffered(1)),
        out_shape=jax.ShapeDtypeStruct((T, D_MODEL), F32),
        scratch_shapes=[pltpu.SMEM((N_EXP,), jnp.int32), pltpu.SMEM((2 * tt,), jnp.int32),
                        pltpu.SMEM((2 * tt,), F32), pltpu.VMEM((MOE_ROWS, D_MODEL), F32),
                        pltpu.VMEM((MOE_ROWS, D_MODEL), F32)],
        compiler_params=pltpu.CompilerParams(dimension_semantics=("arbitrary", "arbitrary"),
                                             vmem_limit_bytes=VMEM_LIMIT),
        name="moe",
    )(idx, gts, cnt, h2, wg, wu, wd)


def _final_kernel(x1_ref, y_ref, g_ref, o_ref):
    o_ref[...] = _rms(x1_ref[...] + y_ref[...], g_ref[...])


def _final_call(x1, y, g):
    T = x1.shape[0]
    tm = 512
    rows = pl.BlockSpec((tm, D_MODEL), lambda i: (i, 0))
    return pl.pallas_call(
        _final_kernel,
        grid=(T // tm,),
        in_specs=[rows, rows, pl.BlockSpec((1, D_MODEL), lambda i: (0, 0))],
        out_specs=rows,
        out_shape=jax.ShapeDtypeStruct((T, D_MODEL), F32),
        compiler_params=pltpu.CompilerParams(dimension_semantics=("arbitrary",)),
        name="final",
    )(x1, y, g)


def _layer(x, attn_norm_g, w_in, b_gate, rwkv_mu, rwkv_w0, rwkv_w_decay, rwkv_a0, rwkv_w_a, rwkv_w_g,
           rwkv_k_k, rwkv_k_a, rwkv_r_k, rwkv_ln_g, rwkv_ln_b, w_up_rwkv, ssd_conv_w, ssd_conv_b,
           ssd_dt_bias, ssd_a_log, ssd_d, ssd_norm_g, w_up_ssd, w_out, ffn_norm_g, w_router_group,
           b_router_group, w_router_expert, b_router_expert, w_exp_gate, w_exp_up, w_exp_down):
    B, S, _ = x.shape
    T = B * S
    row = lambda a: a.reshape(1, -1).astype(F32)
    ng = row(attn_norm_g)

    w_rw = w_in[:, :RW_COLS].astype(BF16)
    wl = jnp.zeros((LANES, 2 * RW), F32)
    wl = wl.at[:DECAY_LORA, :RW].set(rwkv_w_decay).at[DECAY_LORA:, RW:].set(rwkv_w_a).astype(BF16)
    y_a = _rwkv_call(x, ng, w_rw, row(rwkv_mu), row(rwkv_w0), wl, row(rwkv_a0), rwkv_w_g.astype(BF16),
                     row(rwkv_k_k), row(rwkv_k_a), row(rwkv_r_k), row(rwkv_ln_g), row(rwkv_ln_b))

    n_ssd = 2 * SW + 2 * S_GROUPS * S_STATE + S_HEADS
    w_ssd = jnp.pad(w_in[:, RW_COLS:RW_COLS + n_ssd], ((0, 0), (0, S_COLS_PAD - n_ssd))).astype(BF16)
    pad_h = lambda a: jnp.pad(a.reshape(1, -1).astype(F32), ((0, 0), (0, LANES - S_HEADS)))
    expand = (jnp.arange(LANES)[:, None] == (jnp.arange(SW) // HEAD)[None, :]).astype(BF16)
    y_b = _ssd_call(x, ng, w_ssd, ssd_conv_w.astype(F32), row(ssd_conv_b), pad_h(ssd_dt_bias), pad_h(ssd_a_log),
                    row(jnp.repeat(ssd_d, HEAD)), row(ssd_norm_g), expand)

    moe_tile = min(MOE_TILE, T)
    w_gt = w_in[:, RW_COLS + n_ssd:].astype(BF16)
    w_r = jnp.zeros((D_MODEL, LANES), F32)
    w_r = w_r.at[:, :N_GROUPS].set(w_router_group).at[:, ROUTE_E0:ROUTE_E0 + N_EXP].set(w_router_expert)
    w_rh = w_r.astype(BF16)
    w_rl = (w_r - w_rh.astype(F32)).astype(BF16)
    b_r = jnp.zeros((1, LANES), F32)
    b_r = b_r.at[0, :N_GROUPS].set(b_router_group).at[0, ROUTE_E0:ROUTE_E0 + N_EXP].set(b_router_expert)
    x1, h2, route, cnt = _merge_call(
        x.reshape(T, D_MODEL), y_a.reshape(T, RW), y_b.reshape(T, SW), ng, w_gt, row(b_gate),
        w_up_rwkv.astype(BF16), w_up_ssd.astype(BF16), w_out.astype(BF16), row(ffn_norm_g), w_rh, w_rl, b_r,
        moe_tile)

    n_moe = T // moe_tile
    idx = route[:, 0:4].astype(jnp.int32).reshape(n_moe, moe_tile, 4).transpose(0, 2, 1).reshape(n_moe, 1, 4 * moe_tile)
    gts = route[:, 4:6].reshape(n_moe, moe_tile, 2).transpose(0, 2, 1).reshape(n_moe, 1, 2 * moe_tile)
    counts = cnt[:, 0:1, ROUTE_E0:ROUTE_E0 + N_EXP].astype(jnp.int32)
    y_moe = _moe_call(idx, gts, counts, h2, w_exp_gate.astype(BF16), w_exp_up.astype(BF16),
                      w_exp_down.astype(BF16), moe_tile)
    return x1, y_moe


def kernel(x, attn_norm_g, w_in, b_gate, rwkv_mu, rwkv_w0, rwkv_w_decay, rwkv_a0, rwkv_w_a, rwkv_w_g, rwkv_k_k, rwkv_k_a, rwkv_r_k, rwkv_ln_g, rwkv_ln_b, w_up_rwkv, ssd_conv_w, ssd_conv_b, ssd_dt_bias, ssd_a_log, ssd_d, ssd_norm_g, w_up_ssd, w_out, ffn_norm_g, w_router_group, b_router_group, w_router_expert, b_router_expert, w_exp_gate, w_exp_up, w_exp_down, final_norm_g):
    B, S, _ = x.shape
    depth = attn_norm_g.shape[0]
    assert depth == 1, "the final residual add is fused with the last layer's MoE output"
    layer_params = (attn_norm_g, w_in, b_gate, rwkv_mu, rwkv_w0, rwkv_w_decay, rwkv_a0, rwkv_w_a, rwkv_w_g,
                    rwkv_k_k, rwkv_k_a, rwkv_r_k, rwkv_ln_g, rwkv_ln_b, w_up_rwkv, ssd_conv_w, ssd_conv_b,
                    ssd_dt_bias, ssd_a_log, ssd_d, ssd_norm_g, w_up_ssd, w_out, ffn_norm_g, w_router_group,
                    b_router_group, w_router_expert, b_router_expert, w_exp_gate, w_exp_up, w_exp_down)
    x1, y_moe = _layer(x, *(prm[0] for prm in layer_params))
    out = _final_call(x1, y_moe, final_norm_g.reshape(1, -1).astype(F32))
    return out.reshape(B, S, D_MODEL)
```

```python
import functools

import jax
import jax.numpy as jnp
from jax import lax
from jax.experimental import pallas as pl
from jax.experimental.pallas import tpu as pltpu

F32 = jnp.float32
BF16 = jnp.bfloat16

D_MODEL = 1024
NORM_EPS = 1e-6
LANES = 128
HEAD = 64
PAIR = 2 * HEAD

RW = 1024
RW_SLAB = PAIR
RW_SLABS = RW // RW_SLAB
DECAY_LORA = 64
AAA_LORA = 64
GATE_LORA = 128
RW_COLS = 3 * RW + DECAY_LORA + AAA_LORA + GATE_LORA
RW_GN_EPS = 64e-5
RW_CHUNK = 64
RW_GROUP = 4
RW_TB = 256
EXP_NEG_HALF = 0.6065306597126334

SW = 2048
S_HEADS = SW // HEAD
S_PAIRS = SW // PAIR
S_GROUPS = 4
S_STATE = 128
S_CONV = 4
S_XBC = SW + 2 * S_GROUPS * S_STATE
S_COLS_PAD = SW + S_XBC + LANES
S_CHUNK = 128
S_TB = 256
PAIRS_PER_GROUP = S_PAIRS // S_GROUPS

N_GROUPS = 4
EPG = 8
N_EXP = 32
D_EXP = 512
MOE_ROWS = 320
MOE_TILE = 4096
MOE_UNROLL = 16
MERGE_TM = 512
ROUTE_E0 = N_GROUPS

VMEM_LIMIT = 56 * 1024 * 1024


def _dot(a, b):
    return jnp.dot(a.astype(BF16), b.astype(BF16), preferred_element_type=F32)


def _dot_nt(a, b):
    return lax.dot_general(a.astype(BF16), b.astype(BF16), (((1,), (1,)), ((), ())),
                           preferred_element_type=F32)


def _split(x):
    hi = x.astype(BF16)
    lo = (x - hi.astype(F32)).astype(BF16)
    return hi, lo


def _dot_sel_l(sel, x):
    hi, lo = _split(x)
    return (jnp.dot(sel, hi, preferred_element_type=F32)
            + jnp.dot(sel, lo, preferred_element_type=F32))


def _dot_sel_r(x, sel):
    hi, lo = _split(x)
    return (jnp.dot(hi, sel, preferred_element_type=F32)
            + jnp.dot(lo, sel, preferred_element_type=F32))


def _iota(shape, dim):
    return lax.broadcasted_iota(jnp.int32, shape, dim)


def _softplus(x):
    return jnp.maximum(x, 0.0) + jnp.log(1.0 + jnp.exp(-jnp.abs(x)))


def _sigmoid(x):
    return 1.0 / (1.0 + jnp.exp(-x))


def _rms(x, g):
    return x * lax.rsqrt(jnp.mean(x * x, axis=-1, keepdims=True) + NORM_EPS) * g


def _block_diag(y):
    head = _iota(y.shape, 1) // HEAD
    yb = y.astype(BF16)
    zero = jnp.zeros_like(yb)
    return jnp.concatenate([jnp.where(head == i, yb, zero) for i in range(y.shape[1] // HEAD)], axis=0)


def _seg_sum(x, seg_ones, split=True):
    outs = []
    w = seg_ones.shape[0]
    for c in range(x.shape[1] // w):
        xc = x[:, c * w:(c + 1) * w]
        outs.append(_dot_sel_r(xc, seg_ones) if split else _dot(xc, seg_ones))
    return jnp.concatenate(outs, axis=1)


def _rwkv_kernel(x_ref, ng_ref, w_ref, mu_ref, w0_ref, wl_ref, a0_ref, wg_ref, kk_ref, ka_ref, rk_ref,
                 lng_ref, lnb_ref, o_ref,
                 prev_ref, st_ref, r_s, k_s, kk_s, b_s, v_s, ld_s, y_s):
    tb = x_ref.shape[0]
    L = RW_CHUNK

    @pl.when(pl.program_id(1) == 0)
    def _():
        prev_ref[...] = jnp.zeros_like(prev_ref)
        st_ref[...] = jnp.zeros_like(st_ref)

    h = _rms(x_ref[...], ng_ref[...])
    p = jnp.dot(h.astype(BF16), w_ref[...], preferred_element_type=F32)

    rolled = pltpu.roll(p, 1, 0)
    top = jnp.where(_iota((8, RW_COLS), 0) == 0, prev_ref[7:8, :], rolled[0:8, :])
    shifted = jnp.concatenate([top, rolled[8:, :]], axis=0)
    prev_ref[...] = p[tb - 8:tb, :]
    u = p + mu_ref[...] * (shifted - p)

    r = u[:, 0:RW]
    k = u[:, RW:2 * RW]
    v = u[:, 2 * RW:3 * RW]
    zwa = u[:, 3 * RW:3 * RW + LANES]
    zg = u[:, 3 * RW + LANES:3 * RW + 2 * LANES]
    lora_in = jnp.where(_iota(zwa.shape, 1) < DECAY_LORA, jnp.tanh(zwa), zwa)
    dl = _dot(lora_in, wl_ref[...])
    ld = -EXP_NEG_HALF * _sigmoid(w0_ref[...] + dl[:, :RW])
    a = _sigmoid(a0_ref[...] + dl[:, RW:])
    gate = _dot(_sigmoid(zg), wg_ref[...])

    seg_ones = (_iota((2 * PAIR, 2 * PAIR), 0) // HEAD == _iota((2 * PAIR, 2 * PAIR), 1) // HEAD).astype(BF16)
    kk = k * kk_ref[...]
    kk = kk * lax.rsqrt(jnp.maximum(_seg_sum(kk * kk, seg_ones, split=False), 1e-24))
    k2 = k * (1.0 + (a - 1.0) * ka_ref[...])
    bonus = _seg_sum(r * k2 * rk_ref[...], seg_ones) * v

    r_s[...] = r
    k_s[...] = k2
    kk_s[...] = kk
    b_s[...] = kk * a
    v_s[...] = v
    ld_s[...] = ld

    W = RW_SLAB
    t_i = _iota((L, W), 0)
    s_i = _iota((L, W), 1) % HEAD
    strict = s_i < t_i
    incl = s_i <= t_i
    eye = (s_i == t_i).astype(F32)
    tril_l = (_iota((L, L), 1) <= _iota((L, L), 0)).astype(BF16)
    bd_mask = (_iota((W, W), 0) // HEAD) == (_iota((W, W), 1) // HEAD)

    def level_mask(b):
        same = (t_i // (2 * b)) == (s_i // (2 * b))
        return same & ((t_i // b) % 2 == 1) & ((s_i // b) % 2 == 0)

    prs = range(RW_SLABS)
    sls = [slice(pr * W, (pr + 1) * W) for pr in prs]

    def group(cg, carry):
        rows, a_t, r_t, b_t, k_t, b_h, k_h, vc, g_end = [], [], [], [], [], [], [], [], []
        for j in range(RW_GROUP):
            rj = pl.ds(pl.multiple_of((cg * RW_GROUP + j) * L, L), L)
            ldc = ld_s[rj, :]
            cs = _dot_sel_l(tril_l, ldc)
            cs_last = cs[L - 1:L, :]
            e_inv = jnp.exp(-cs)
            e_end = jnp.exp(cs_last - cs)
            bc = b_s[rj, :]
            kc = k_s[rj, :]
            rows.append(rj)
            a_t.append(-kk_s[rj, :] * jnp.exp(cs - ldc))
            r_t.append(r_s[rj, :] * jnp.exp(cs))
            b_t.append(bc * e_inv)
            k_t.append(kc * e_inv)
            b_h.append(bc * e_end)
            k_h.append(kc * e_end)
            vc.append(v_s[rj, :])
            g_end.append(jnp.exp(cs_last))

        items = [(j, pr) for j in range(RW_GROUP) for pr in prs]
        g = [_dot_nt(jnp.concatenate([a_t[j][:, sls[pr]], r_t[j][:, sls[pr]]], axis=0),
                     jnp.concatenate([_block_diag(b_t[j][:, sls[pr]]), _block_diag(k_t[j][:, sls[pr]])], axis=0))
             for j, pr in items]
        n_ab = [jnp.where(strict, gi[0:L, 0:W], 0.0) for gi in g]
        a_ak = [jnp.where(strict, gi[0:L, W:], 0.0) for gi in g]
        m_rb = [jnp.where(incl, gi[L:, 0:W], 0.0) for gi in g]
        m_rk = [jnp.where(incl, gi[L:, W:], 0.0) for gi in g]
        n_it = range(len(items))

        lm = level_mask(1)
        xinv = [eye + jnp.where(lm, n_ab[i], 0.0) for i in n_it]
        b = 2
        while b < L:
            lm = level_mask(b)
            t1 = [_dot(xinv[i], _block_diag(jnp.where(lm, n_ab[i], 0.0))) for i in n_it]
            q = [_dot(t1[i], _block_diag(xinv[i])) for i in n_it]
            xinv = [xinv[i] + q[i] for i in n_it]
            b *= 2

        v_bd = [_block_diag(vc[j][:, sls[pr]]) for j, pr in items]
        akv = [_dot(a_ak[i], v_bd[i]) for i in n_it]
        wu = [_dot(xinv[i], jnp.concatenate([_block_diag(a_t[j][:, sls[pr]]), _block_diag(akv[i])], axis=1))
              for i, (j, pr) in enumerate(items)]
        y_loc = [_dot(m_rk[i], v_bd[i]) for i in n_it]

        for j in range(RW_GROUP):
            it = [j * RW_SLABS + pr for pr in prs]
            s_bd = [st_ref[pr] for pr in prs]
            u_c = [_dot_nt(wu[it[pr]][:, 0:W], s_bd[pr]) + wu[it[pr]][:, W:] for pr in prs]
            y_r = [_dot_nt(r_t[j][:, sls[pr]], s_bd[pr]) for pr in prs]
            y_c = [y_r[pr] + _dot(m_rb[it[pr]], _block_diag(u_c[pr])) + y_loc[it[pr]] for pr in prs]
            upd = [_dot(jnp.concatenate([u_c[pr], vc[j][:, sls[pr]]], axis=0).T,
                        jnp.concatenate([b_h[j][:, sls[pr]], k_h[j][:, sls[pr]]], axis=0)) for pr in prs]
            for pr in prs:
                st_ref[pr] = s_bd[pr] * g_end[j][:, sls[pr]] + jnp.where(bd_mask, upd[pr], 0.0)
                y_s[rows[j], sls[pr]] = y_c[pr]
        return carry

    lax.fori_loop(0, tb // (L * RW_GROUP), group, 0)

    y = y_s[...]
    mean = _seg_sum(y, seg_ones) * (1.0 / HEAD)
    yc = y - mean
    var = _seg_sum(yc * yc, seg_ones, split=False) * (1.0 / HEAD)
    yn = yc * lax.rsqrt(var + RW_GN_EPS) * lng_ref[...] + lnb_ref[...]
    o_ref[...] = ((yn + bonus) * gate).astype(o_ref.dtype)


def _rwkv_call(x, ng, w, mu, w0, wl, a0, wg, k_k, k_a, r_k, ln_g, ln_b):
    B, S, _ = x.shape
    tb = RW_TB
    full = lambda shape: pl.BlockSpec(shape, lambda b, i: (0,) * len(shape))
    row = full((1, RW))
    return pl.pallas_call(
        _rwkv_kernel,
        grid=(B, S // tb),
        in_specs=[pl.BlockSpec((None, tb, D_MODEL), lambda b, i: (b, i, 0)),
                  full((1, D_MODEL)), full((D_MODEL, RW_COLS)), full((1, RW_COLS)), row,
                  full((LANES, 2 * RW)), row, full((GATE_LORA, RW)), row, row, row, row, row],
        out_specs=pl.BlockSpec((None, tb, RW), lambda b, i: (b, i, 0)),
        out_shape=jax.ShapeDtypeStruct((B, S, RW), BF16),
        scratch_shapes=[pltpu.VMEM((8, RW_COLS), F32), pltpu.VMEM((RW_SLABS, RW_SLAB, RW_SLAB), F32)]
        + [pltpu.VMEM((tb, RW), F32)] * 7,
        compiler_params=pltpu.CompilerParams(dimension_semantics=("arbitrary", "arbitrary"),
                                             vmem_limit_bytes=VMEM_LIMIT),
        name="rwkv",
    )(x, ng, w, mu, w0, wl, a0, wg, k_k, k_a, r_k, ln_g, ln_b)


def _ssd_kernel(x_ref, ng_ref, w_ref, cw_ref, cb_ref, dtb_ref, alog_ref, dx_ref, sng_ref, exp_ref, o_ref,
                prev_ref, st_ref, xs_s, b_s, c_s, acs_s, acst_s, dtt_s, y_s):
    tb = x_ref.shape[0]
    L = S_CHUNK

    @pl.when(pl.program_id(1) == 0)
    def _():
        prev_ref[...] = jnp.zeros_like(prev_ref)
        st_ref[...] = jnp.zeros_like(st_ref)

    h = _rms(x_ref[...], ng_ref[...])
    p = jnp.dot(h.astype(BF16), w_ref[...], preferred_element_type=F32)
    z = p[:, 0:SW]
    xbc = p[:, SW:SW + S_XBC]
    dt_raw = p[:, SW + S_XBC:]

    carry = prev_ref[...]
    prev_ref[...] = xbc[tb - 8:tb, :]
    row8 = _iota((8, S_XBC), 0)
    conv = cb_ref[...] + cw_ref[S_CONV - 1:S_CONV, :] * xbc
    for j in range(1, S_CONV):
        sh = pltpu.roll(xbc, j, 0)
        top = jnp.where(row8 < j, pltpu.roll(carry, j, 0), sh[0:8, :])
        sh = jnp.concatenate([top, sh[8:, :]], axis=0)
        conv = conv + cw_ref[S_CONV - 1 - j:S_CONV - j, :] * sh
    act = conv * _sigmoid(conv)
    xs = act[:, 0:SW]
    b_s[...] = act[:, SW:SW + S_GROUPS * S_STATE]
    c_s[...] = act[:, SW + S_GROUPS * S_STATE:]

    head_lane = _iota((tb, LANES), 1) < S_HEADS
    dt = jnp.where(head_lane, _softplus(dt_raw + dtb_ref[...]), 0.0)
    a_dt = dt * (-jnp.exp(alog_ref[...]))
    expand = exp_ref[...]
    xs_s[...] = xs

    ri = _iota((tb, tb), 0)
    ci = _iota((tb, tb), 1)
    same_chunk = (ri // L) == (ci // L)
    tril_blk = (same_chunk & (ci <= ri)).astype(BF16)
    triu_blk = (same_chunk & (ri <= ci)).astype(BF16)
    acs = _dot_sel_l(tril_blk, a_dt)
    acs_s[...] = acs
    acst = _dot_sel_r(a_dt.T, triu_blk)
    dtt = dt.T
    for c in range(tb // L):
        acst_s[c] = acst[:, c * L:(c + 1) * L]
        dtt_s[c] = dtt[:, c * L:(c + 1) * L]

    tril_ll = _iota((L, L), 1) <= _iota((L, L), 0)

    def chunk(c, carry_):
        r0 = pl.multiple_of(c * L, L)
        rows = pl.ds(r0, L)
        acs_c = acs_s[rows, :]
        acs_t = acst_s[c]
        dt_t = dtt_s[c]
        a_last = acs_c[L - 1:L, :]
        e_end = _dot_sel_r(jnp.broadcast_to(jnp.exp(a_last), (8, LANES)), expand)[0:1, :]
        xs_c = xs_s[rows, :]
        for g in range(S_GROUPS):
            gs = slice(g * S_STATE, (g + 1) * S_STATE)
            b_g = b_s[rows, gs]
            c_g = c_s[rows, gs]
            cb = _dot_nt(c_g, b_g)
            b_gt = b_g.T
            for q in range(PAIRS_PER_GROUP):
                pr = g * PAIRS_PER_GROUP + q
                sl = slice(pr * PAIR, (pr + 1) * PAIR)
                ms, cs_, bw = [], [], []
                for hh in (2 * pr, 2 * pr + 1):
                    col = jnp.broadcast_to(acs_c[:, hh:hh + 1], (L, L))
                    row = acs_t[hh:hh + 1, :] - jnp.log(dt_t[hh:hh + 1, :])
                    ms.append(cb * jnp.exp(jnp.where(tril_ll, col - row, -jnp.inf)))
                    cs_.append(c_g * jnp.exp(col))
                    bw.append(b_gt * jnp.exp(acs_t[hh:hh + 1, L - 1:L] - row))
                xs_bd = _block_diag(xs_c[:, sl])
                s_p = st_ref[pr]
                y_s[rows, sl] = _dot(jnp.concatenate(ms + cs_, axis=1),
                                     jnp.concatenate([xs_bd, _block_diag(s_p)], axis=0))
                st_ref[pr] = s_p * e_end[:, sl] + _dot(jnp.concatenate(bw, axis=1), xs_bd)
        return carry_

    lax.fori_loop(0, tb // L, chunk, 0)

    y = y_s[...] + xs * dx_ref[...]
    uu = y * (z * _sigmoid(z))
    gw = SW // S_GROUPS
    outs = []
    for g in range(S_GROUPS):
        ug = uu[:, g * gw:(g + 1) * gw]
        outs.append(ug * lax.rsqrt(jnp.mean(ug * ug, axis=-1, keepdims=True) + NORM_EPS))
    o_ref[...] = (jnp.concatenate(outs, axis=1) * sng_ref[...]).astype(o_ref.dtype)


def _ssd_call(x, ng, w, cw, cb, dtb, alog, dx, sng, expand):
    B, S, _ = x.shape
    tb = S_TB
    full = lambda shape: pl.BlockSpec(shape, lambda b, i: (0,) * len(shape))
    return pl.pallas_call(
        _ssd_kernel,
        grid=(B, S // tb),
        in_specs=[pl.BlockSpec((None, tb, D_MODEL), lambda b, i: (b, i, 0)),
                  full((1, D_MODEL)), full((D_MODEL, S_COLS_PAD)), full((S_CONV, S_XBC)), full((1, S_XBC)),
                  full((1, LANES)), full((1, LANES)), full((1, SW)), full((1, SW)), full((LANES, SW))],
        out_specs=pl.BlockSpec((None, tb, SW), lambda b, i: (b, i, 0)),
        out_shape=jax.ShapeDtypeStruct((B, S, SW), BF16),
        scratch_shapes=[pltpu.VMEM((8, S_XBC), F32), pltpu.VMEM((S_PAIRS, S_STATE, PAIR), F32),
                        pltpu.VMEM((tb, SW), F32),
                        pltpu.VMEM((tb, S_GROUPS * S_STATE), F32), pltpu.VMEM((tb, S_GROUPS * S_STATE), F32),
                        pltpu.VMEM((tb, LANES), F32), pltpu.VMEM((tb // S_CHUNK, LANES, S_CHUNK), F32),
                        pltpu.VMEM((tb // S_CHUNK, LANES, S_CHUNK), F32), pltpu.VMEM((tb, SW), F32)],
        compiler_params=pltpu.CompilerParams(dimension_semantics=("arbitrary", "arbitrary"),
                                             vmem_limit_bytes=VMEM_LIMIT),
        name="ssd",
    )(x, ng, w, cw, cb, dtb, alog, dx, sng, expand)


def _merge_kernel(x_ref, ya_ref, yb_ref, ng_ref, wgt_ref, bgt_ref, wa_ref, wb_ref, wo_ref, fg_ref,
                  wrh_ref, wrl_ref, br_ref, x1_ref, h2_ref, rt_ref, cnt_ref, carry_ref, *, tiles_per_moe):
    tm = x_ref.shape[0]
    i = pl.program_id(0)

    @pl.when(i % tiles_per_moe == 0)
    def _():
        carry_ref[...] = jnp.zeros_like(carry_ref)

    x = x_ref[...]
    h = _rms(x, ng_ref[...])
    gates = _sigmoid(jnp.dot(h.astype(BF16), wgt_ref[...], preferred_element_type=F32) + bgt_ref[...])
    up_a = jnp.dot(ya_ref[...], wa_ref[...], preferred_element_type=F32)
    up_b = jnp.dot(yb_ref[...], wb_ref[...], preferred_element_type=F32)
    merged = gates[:, :D_MODEL] * up_a + gates[:, D_MODEL:] * up_b
    x1 = x + _dot(merged, wo_ref[...])
    x1_ref[...] = x1
    h2 = _rms(x1, fg_ref[...])
    h2_ref[...] = h2

    hh, hl = _split(h2)
    wh, wl = wrh_ref[...], wrl_ref[...]
    logits = (jnp.dot(hh, wh, preferred_element_type=F32) + jnp.dot(hh, wl, preferred_element_type=F32)
              + jnp.dot(hl, wh, preferred_element_type=F32)) + br_ref[...]
    li = _iota(logits.shape, 1).astype(F32)
    neg = -jnp.inf
    big = float(LANES)

    gl = jnp.where(li < N_GROUPS, logits, neg)
    gmax = jnp.max(gl, axis=-1, keepdims=True)
    grp = jnp.min(jnp.where(gl == gmax, li, big), axis=-1, keepdims=True)
    g_w = 1.0 / jnp.sum(jnp.exp(gl - gmax), axis=-1, keepdims=True)

    lo_lane = ROUTE_E0 + EPG * grp
    el = jnp.where((li >= lo_lane) & (li < lo_lane + EPG), logits, neg)
    emax = jnp.max(el, axis=-1, keepdims=True)
    i1 = jnp.min(jnp.where(el == emax, li, big), axis=-1, keepdims=True)
    esum = jnp.sum(jnp.exp(el - emax), axis=-1, keepdims=True)
    el2 = jnp.where(li == i1, neg, el)
    m2 = jnp.max(el2, axis=-1, keepdims=True)
    i2 = jnp.min(jnp.where(el2 == m2, li, big), axis=-1, keepdims=True)
    p1 = 1.0 / esum
    p2 = jnp.exp(m2 - emax) / esum
    gate1 = g_w * p1 / (p1 + p2)
    gate2 = g_w * p2 / (p1 + p2)

    hit1 = li == i1
    hit2 = li == i2
    onehot = jnp.where(hit1 | hit2, 1.0, 0.0)
    tril_strict = (_iota((tm, tm), 1) < _iota((tm, tm), 0)).astype(BF16)
    cum = jnp.dot(tril_strict, onehot.astype(BF16), preferred_element_type=F32) + carry_ref[0:1, :]
    r1 = jnp.sum(jnp.where(hit1, cum, 0.0), axis=-1, keepdims=True)
    r2 = jnp.sum(jnp.where(hit2, cum, 0.0), axis=-1, keepdims=True)
    carry_ref[...] = carry_ref[...] + jnp.sum(onehot, axis=0, keepdims=True)
    cnt_ref[...] = carry_ref[...]

    e1 = i1 - ROUTE_E0
    e2 = i2 - ROUTE_E0
    rt = jnp.zeros(logits.shape, F32)
    for lane, val in enumerate((e1, e2, r1, r2, gate1, gate2)):
        rt = jnp.where(li == lane, val, rt)
    rt_ref[...] = rt


def _merge_call(x2, ya, yb, ng, wgt, bgt, wa, wb, wo, fg, wrh, wrl, br, moe_tile):
    T = x2.shape[0]
    tm = MERGE_TM
    tiles_per_moe = moe_tile // tm
    n_moe = T // moe_tile
    full = lambda shape: pl.BlockSpec(shape, lambda i: (0,) * len(shape))
    rows = lambda w: pl.BlockSpec((tm, w), lambda i: (i, 0))
    return pl.pallas_call(
        functools.partial(_merge_kernel, tiles_per_moe=tiles_per_moe),
        grid=(T // tm,),
        in_specs=[rows(D_MODEL), rows(RW), rows(SW), full((1, D_MODEL)), full((D_MODEL, 2 * D_MODEL)),
                  full((1, 2 * D_MODEL)), full((RW, D_MODEL)), full((SW, D_MODEL)), full((D_MODEL, D_MODEL)),
                  full((1, D_MODEL)), full((D_MODEL, LANES)), full((D_MODEL, LANES)), full((1, LANES))],
        out_specs=[rows(D_MODEL), rows(D_MODEL), rows(LANES),
                   pl.BlockSpec((None, 8, LANES), lambda i: (i // tiles_per_moe, 0, 0))],
        out_shape=[jax.ShapeDtypeStruct((T, D_MODEL), F32), jax.ShapeDtypeStruct((T, D_MODEL), F32),
                   jax.ShapeDtypeStruct((T, LANES), F32), jax.ShapeDtypeStruct((n_moe, 8, LANES), F32)],
        scratch_shapes=[pltpu.VMEM((8, LANES), F32)],
        compiler_params=pltpu.CompilerParams(dimension_semantics=("arbitrary",),
                                             vmem_limit_bytes=VMEM_LIMIT),
        name="merge",
    )(x2, ya, yb, ng, wgt, bgt, wa, wb, wo, fg, wrh, wrl, br)


def _moe_kernel(idx_ref, gts_ref, cnt_ref, h2_ref, wg_ref, wu_ref, wd_ref, y_ref,
                start_ref, stok_ref, sg_ref, xb_ref, ob_ref):
    e = pl.program_id(1)
    tt = h2_ref.shape[0]

    @pl.when(e == 0)
    def _():
        def starts(j, acc):
            start_ref[j] = acc
            return acc + cnt_ref[0, j]
        lax.fori_loop(0, N_EXP, starts, jnp.int32(0))

        def place(j, c):
            for u in range(MOE_UNROLL):
                t = j * MOE_UNROLL + u
                for k in range(2):
                    slot = start_ref[idx_ref[0, k * tt + t]] + idx_ref[0, (2 + k) * tt + t]
                    stok_ref[slot] = t
                    sg_ref[slot] = gts_ref[0, k * tt + t]
            return c
        lax.fori_loop(0, tt // MOE_UNROLL, place, 0)
        y_ref[...] = jnp.zeros_like(y_ref)
        xb_ref[...] = jnp.zeros_like(xb_ref)

    n = cnt_ref[0, e]
    s0 = start_ref[e]
    last_slot = 2 * tt - 1

    def block(b, c):
        base = s0 + b * MOE_ROWS
        m = jnp.minimum(n - b * MOE_ROWS, MOE_ROWS)
        groups = (m + MOE_UNROLL - 1) // MOE_UNROLL

        def gather(j, c2):
            dst = xb_ref.at[pl.ds(pl.multiple_of(j * MOE_UNROLL, MOE_UNROLL), MOE_UNROLL)]
            for u in range(MOE_UNROLL):
                t = stok_ref[jnp.minimum(base + j * MOE_UNROLL + u, last_slot)]
                dst[u:u + 1, :] = h2_ref[pl.ds(t, 1), :]
            return c2
        lax.fori_loop(0, groups, gather, 0)

        xb = xb_ref[...].astype(BF16)
        hg = jnp.dot(xb, wg_ref[...], preferred_element_type=F32)
        hu = jnp.dot(xb, wu_ref[...], preferred_element_type=F32)
        hid = hg * _sigmoid(hg) * hu
        ob_ref[...] = jnp.dot(hid.astype(BF16), wd_ref[...], preferred_element_type=F32)

        def scatter(j, c2):
            done = []
            src = ob_ref.at[pl.ds(pl.multiple_of(j * MOE_UNROLL, MOE_UNROLL), MOE_UNROLL)]
            for u in range(MOE_UNROLL):
                slot = base + j * MOE_UNROLL + u
                t = stok_ref[slot]
                done.append((t, y_ref[pl.ds(t, 1), :] + sg_ref[slot] * src[u:u + 1, :]))
            for t, val in done:
                y_ref[pl.ds(t, 1), :] = val
            return c2
        full = m // MOE_UNROLL
        lax.fori_loop(0, full, scatter, 0)

        def scatter_row(r, c2):
            t = stok_ref[base + r]
            y_ref[pl.ds(t, 1), :] = y_ref[pl.ds(t, 1), :] + sg_ref[base + r] * ob_ref[pl.ds(r, 1), :]
            return c2
        lax.fori_loop(full * MOE_UNROLL, m, scatter_row, 0)
        return c

    lax.fori_loop(0, (n + MOE_ROWS - 1) // MOE_ROWS, block, 0)


def _moe_call(idx, gts, cnt, h2, wg, wu, wd, moe_tile):
    T = h2.shape[0]
    tt = moe_tile
    smem = lambda w: pl.BlockSpec((None, 1, w), lambda i, e: (i, 0, 0), memory_space=pltpu.SMEM)
    return pl.pallas_call(
        _moe_kernel,
        grid=(T // tt, N_EXP),
        in_specs=[smem(4 * tt), smem(2 * tt), smem(N_EXP),
                  pl.BlockSpec((tt, D_MODEL), lambda i, e: (i, 0), pipeline_mode=pl.Buffered(1)),
                  pl.BlockSpec((None, D_MODEL, D_EXP), lambda i, e: (e, 0, 0)),
                  pl.BlockSpec((None, D_MODEL, D_EXP), lambda i, e: (e, 0, 0)),
                  pl.BlockSpec((None, D_EXP, D_MODEL), lambda i, e: (e, 0, 0))],
        out_specs=pl.BlockSpec((tt, D_MODEL), lambda i, e: (i, 0), pipeline_mode=pl.Buffered(1)),
        out_shape=jax.ShapeDtypeStruct((T, D_MODEL), F32),
        scratch_shapes=[pltpu.SMEM((N_EXP,), jnp.int32), pltpu.SMEM((2 * tt,), jnp.int32),
                        pltpu.SMEM((2 * tt,), F32), pltpu.VMEM((MOE_ROWS, D_MODEL), F32),
                        pltpu.VMEM((MOE_ROWS, D_MODEL), F32)],
        compiler_params=pltpu.CompilerParams(dimension_semantics=("arbitrary", "arbitrary"),
                                             vmem_limit_bytes=VMEM_LIMIT),
        name="moe",
    )(idx, gts, cnt, h2, wg, wu, wd)


def _final_kernel(x1_ref, y_ref, g_ref, o_ref):
    o_ref[...] = _rms(x1_ref[...] + y_ref[...], g_ref[...])


def _final_call(x1, y, g):
    T = x1.shape[0]
    tm = 512
    rows = pl.BlockSpec((tm, D_MODEL), lambda i: (i, 0))
    return pl.pallas_call(
        _final_kernel,
        grid=(T // tm,),
        in_specs=[rows, rows, pl.BlockSpec((1, D_MODEL), lambda i: (0, 0))],
        out_specs=rows,
        out_shape=jax.ShapeDtypeStruct((T, D_MODEL), F32),
        compiler_params=pltpu.CompilerParams(dimension_semantics=("arbitrary",)),
        name="final",
    )(x1, y, g)


def _layer(x, attn_norm_g, w_in, b_gate, rwkv_mu, rwkv_w0, rwkv_w_decay, rwkv_a0, rwkv_w_a, rwkv_w_g,
           rwkv_k_k, rwkv_k_a, rwkv_r_k, rwkv_ln_g, rwkv_ln_b, w_up_rwkv, ssd_conv_w, ssd_conv_b,
           ssd_dt_bias, ssd_a_log, ssd_d, ssd_norm_g, w_up_ssd, w_out, ffn_norm_g, w_router_group,
           b_router_group, w_router_expert, b_router_expert, w_exp_gate, w_exp_up, w_exp_down):
    B, S, _ = x.shape
    T = B * S
    row = lambda a: a.reshape(1, -1).astype(F32)
    ng = row(attn_norm_g)

    w_rw = w_in[:, :RW_COLS].astype(BF16)
    wl = jnp.zeros((LANES, 2 * RW), F32)
    wl = wl.at[:DECAY_LORA, :RW].set(rwkv_w_decay).at[DECAY_LORA:, RW:].set(rwkv_w_a).astype(BF16)
    y_a = _rwkv_call(x, ng, w_rw, row(rwkv_mu), row(rwkv_w0), wl, row(rwkv_a0), rwkv_w_g.astype(BF16),
                     row(rwkv_k_k), row(rwkv_k_a), row(rwkv_r_k), row(rwkv_ln_g), row(rwkv_ln_b))

    n_ssd = 2 * SW + 2 * S_GROUPS * S_STATE + S_HEADS
    w_ssd = jnp.pad(w_in[:, RW_COLS:RW_COLS + n_ssd], ((0, 0), (0, S_COLS_PAD - n_ssd))).astype(BF16)
    pad_h = lambda a: jnp.pad(a.reshape(1, -1).astype(F32), ((0, 0), (0, LANES - S_HEADS)))
    expand = (jnp.arange(LANES)[:, None] == (jnp.arange(SW) // HEAD)[None, :]).astype(BF16)
    y_b = _ssd_call(x, ng, w_ssd, ssd_conv_w.astype(F32), row(ssd_conv_b), pad_h(ssd_dt_bias), pad_h(ssd_a_log),
                    row(jnp.repeat(ssd_d, HEAD)), row(ssd_norm_g), expand)

    moe_tile = min(MOE_TILE, T)
    w_gt = w_in[:, RW_COLS + n_ssd:].astype(BF16)
    w_r = jnp.zeros((D_MODEL, LANES), F32)
    w_r = w_r.at[:, :N_GROUPS].set(w_router_group).at[:, ROUTE_E0:ROUTE_E0 + N_EXP].set(w_router_expert)
    w_rh = w_r.astype(BF16)
    w_rl = (w_r - w_rh.astype(F32)).astype(BF16)
    b_r = jnp.zeros((1, LANES), F32)
    b_r = b_r.at[0, :N_GROUPS].set(b_router_group).at[0, ROUTE_E0:ROUTE_E0 + N_EXP].set(b_router_expert)
    x1, h2, route, cnt = _merge_call(
        x.reshape(T, D_MODEL), y_a.reshape(T, RW), y_b.reshape(T, SW), ng, w_gt, row(b_gate),
        w_up_rwkv.astype(BF16), w_up_ssd.astype(BF16), w_out.astype(BF16), row(ffn_norm_g), w_rh, w_rl, b_r,
        moe_tile)

    n_moe = T // moe_tile
    idx = route[:, 0:4].astype(jnp.int32).reshape(n_moe, moe_tile, 4).transpose(0, 2, 1).reshape(n_moe, 1, 4 * moe_tile)
    gts = route[:, 4:6].reshape(n_moe, moe_tile, 2).transpose(0, 2, 1).reshape(n_moe, 1, 2 * moe_tile)
    counts = cnt[:, 0:1, ROUTE_E0:ROUTE_E0 + N_EXP].astype(jnp.int32)
    y_moe = _moe_call(idx, gts, counts, h2, w_exp_gate.astype(BF16), w_exp_up.astype(BF16),
                      w_exp_down.astype(BF16), moe_tile)
    return x1, y_moe


def kernel(x, attn_norm_g, w_in, b_gate, rwkv_mu, rwkv_w0, rwkv_w_decay, rwkv_a0, rwkv_w_a, rwkv_w_g, rwkv_k_k, rwkv_k_a, rwkv_r_k, rwkv_ln_g, rwkv_ln_b, w_up_rwkv, ssd_conv_w, ssd_conv_b, ssd_dt_bias, ssd_a_log, ssd_d, ssd_norm_g, w_up_ssd, w_out, ffn_norm_g, w_router_group, b_router_group, w_router_expert, b_router_expert, w_exp_gate, w_exp_up, w_exp_down, final_norm_g):
    B, S, _ = x.shape
    depth = attn_norm_g.shape[0]
    assert depth == 1, "the final residual add is fused with the last layer's MoE output"
    layer_params = (attn_norm_g, w_in, b_gate, rwkv_mu, rwkv_w0, rwkv_w_decay, rwkv_a0, rwkv_w_a, rwkv_w_g,
                    rwkv_k_k, rwkv_k_a, rwkv_r_k, rwkv_ln_g, rwkv_ln_b, w_up_rwkv, ssd_conv_w, ssd_conv_b,
                    ssd_dt_bias, ssd_a_log, ssd_d, ssd_norm_g, w_up_ssd, w_out, ffn_norm_g, w_router_group,
                    b_router_group, w_router_expert, b_router_expert, w_exp_gate, w_exp_up, w_exp_down)
    x1, y_moe = _layer(x, *(prm[0] for prm in layer_params))
    out = _final_call(x1, y_moe, final_norm_g.reshape(1, -1).astype(F32))
    return out.reshape(B, S, D_MODEL)
```

```python
import functools

import jax
import jax.numpy as jnp
from jax import lax
from jax.experimental import pallas as pl
from jax.experimental.pallas import tpu as pltpu

F32 = jnp.float32
BF16 = jnp.bfloat16

D_MODEL = 1024
NORM_EPS = 1e-6
LANES = 128
HEAD = 64
PAIR = 2 * HEAD

RW = 1024
RW_SLAB = PAIR
RW_SLABS = RW // RW_SLAB
DECAY_LORA = 64
AAA_LORA = 64
GATE_LORA = 128
RW_COLS = 3 * RW + DECAY_LORA + AAA_LORA + GATE_LORA
RW_GN_EPS = 64e-5
RW_CHUNK = 64
RW_GROUP = 4
RW_TB = 256
EXP_NEG_HALF = 0.6065306597126334

SW = 2048
S_HEADS = SW // HEAD
S_PAIRS = SW // PAIR
S_GROUPS = 4
S_STATE = 128
S_CONV = 4
S_XBC = SW + 2 * S_GROUPS * S_STATE
S_COLS_PAD = SW + S_XBC + LANES
S_CHUNK = 128
S_TB = 256
PAIRS_PER_GROUP = S_PAIRS // S_GROUPS

N_GROUPS = 4
EPG = 8
N_EXP = 32
D_EXP = 512
MOE_ROWS = 320
MOE_TILE = 4096
MOE_UNROLL = 16
MERGE_TM = 512
ROUTE_E0 = N_GROUPS

VMEM_LIMIT = 56 * 1024 * 1024


def _dot(a, b):
    return jnp.dot(a.astype(BF16), b.astype(BF16), preferred_element_type=F32)


def _dot_nt(a, b):
    return lax.dot_general(a.astype(BF16), b.astype(BF16), (((1,), (1,)), ((), ())),
                           preferred_element_type=F32)


def _split(x):
    hi = x.astype(BF16)
    lo = (x - hi.astype(F32)).astype(BF16)
    return hi, lo


def _dot_sel_l(sel, x):
    hi, lo = _split(x)
    return (jnp.dot(sel, hi, preferred_element_type=F32)
            + jnp.dot(sel, lo, preferred_element_type=F32))


def _dot_sel_r(x, sel):
    hi, lo = _split(x)
    return (jnp.dot(hi, sel, preferred_element_type=F32)
            + jnp.dot(lo, sel, preferred_element_type=F32))


def _iota(shape, dim):
    return lax.broadcasted_iota(jnp.int32, shape, dim)


def _softplus(x):
    return jnp.maximum(x, 0.0) + jnp.log(1.0 + jnp.exp(-jnp.abs(x)))


def _sigmoid(x):
    return 1.0 / (1.0 + jnp.exp(-x))


def _rms(x, g):
    return x * lax.rsqrt(jnp.mean(x * x, axis=-1, keepdims=True) + NORM_EPS) * g


def _block_diag(y):
    head = _iota(y.shape, 1) // HEAD
    yb = y.astype(BF16)
    zero = jnp.zeros_like(yb)
    return jnp.concatenate([jnp.where(head == i, yb, zero) for i in range(y.shape[1] // HEAD)], axis=0)


def _seg_sum(x, seg_ones, split=True):
    outs = []
    w = seg_ones.shape[0]
    for c in range(x.shape[1] // w):
        xc = x[:, c * w:(c + 1) * w]
        outs.append(_dot_sel_r(xc, seg_ones) if split else _dot(xc, seg_ones))
    return jnp.concatenate(outs, axis=1)


def _rwkv_kernel(x_ref, ng_ref, w_ref, mu_ref, w0_ref, wl_ref, a0_ref, wg_ref, kk_ref, ka_ref, rk_ref,
                 lng_ref, lnb_ref, o_ref,
                 prev_ref, st_ref, r_s, k_s, kk_s, b_s, v_s, ld_s, y_s):
    tb = x_ref.shape[0]
    L = RW_CHUNK

    @pl.when(pl.program_id(1) == 0)
    def _():
        prev_ref[...] = jnp.zeros_like(prev_ref)
        st_ref[...] = jnp.zeros_like(st_ref)

    h = _rms(x_ref[...], ng_ref[...])
    p = jnp.dot(h.astype(BF16), w_ref[...], preferred_element_type=F32)

    rolled = pltpu.roll(p, 1, 0)
    top = jnp.where(_iota((8, RW_COLS), 0) == 0, prev_ref[7:8, :], rolled[0:8, :])
    shifted = jnp.concatenate([top, rolled[8:, :]], axis=0)
    prev_ref[...] = p[tb - 8:tb, :]
    u = p + mu_ref[...] * (shifted - p)

    r = u[:, 0:RW]
    k = u[:, RW:2 * RW]
    v = u[:, 2 * RW:3 * RW]
    zwa = u[:, 3 * RW:3 * RW + LANES]
    zg = u[:, 3 * RW + LANES:3 * RW + 2 * LANES]
    lora_in = jnp.where(_iota(zwa.shape, 1) < DECAY_LORA, jnp.tanh(zwa), zwa)
    dl = _dot(lora_in, wl_ref[...])
    ld = -EXP_NEG_HALF * _sigmoid(w0_ref[...] + dl[:, :RW])
    a = _sigmoid(a0_ref[...] + dl[:, RW:])
    gate = _dot(_sigmoid(zg), wg_ref[...])

    seg_ones = (_iota((2 * PAIR, 2 * PAIR), 0) // HEAD == _iota((2 * PAIR, 2 * PAIR), 1) // HEAD).astype(BF16)
    kk = k * kk_ref[...]
    kk = kk * lax.rsqrt(jnp.maximum(_seg_sum(kk * kk, seg_ones, split=False), 1e-24))
    k2 = k * (1.0 + (a - 1.0) * ka_ref[...])
    bonus = _seg_sum(r * k2 * rk_ref[...], seg_ones) * v

    r_s[...] = r
    k_s[...] = k2
    kk_s[...] = kk
    b_s[...] = kk * a
    v_s[...] = v
    ld_s[...] = ld

    W = RW_SLAB
    t_i = _iota((L, W), 0)
    s_i = _iota((L, W), 1) % HEAD
    strict = s_i < t_i
    incl = s_i <= t_i
    eye = (s_i == t_i).astype(F32)
    tril_l = (_iota((L, L), 1) <= _iota((L, L), 0)).astype(BF16)
    bd_mask = (_iota((W, W), 0) // HEAD) == (_iota((W, W), 1) // HEAD)

    def level_mask(b):
        same = (t_i // (2 * b)) == (s_i // (2 * b))
        return same & ((t_i // b) % 2 == 1) & ((s_i // b) % 2 == 0)

    prs = range(RW_SLABS)
    sls = [slice(pr * W, (pr + 1) * W) for pr in prs]

    def group(cg, carry):
        rows, a_t, r_t, b_t, k_t, b_h, k_h, vc, g_end = [], [], [], [], [], [], [], [], []
        for j in range(RW_GROUP):
            rj = pl.ds(pl.multiple_of((cg * RW_GROUP + j) * L, L), L)
            ldc = ld_s[rj, :]
            cs = _dot_sel_l(tril_l, ldc)
            cs_last = cs[L - 1:L, :]
            e_inv = jnp.exp(-cs)
            e_end = jnp.exp(cs_last - cs)
            bc = b_s[rj, :]
            kc = k_s[rj, :]
            rows.append(rj)
            a_t.append(-kk_s[rj, :] * jnp.exp(cs - ldc))
            r_t.append(r_s[rj, :] * jnp.exp(cs))
            b_t.append(bc * e_inv)
            k_t.append(kc * e_inv)
            b_h.append(bc * e_end)
            k_h.append(kc * e_end)
            vc.append(v_s[rj, :])
            g_end.append(jnp.exp(cs_last))

        items = [(j, pr) for j in range(RW_GROUP) for pr in prs]
        g = [_dot_nt(jnp.concatenate([a_t[j][:, sls[pr]], r_t[j][:, sls[pr]]], axis=0),
                     jnp.concatenate([_block_diag(b_t[j][:, sls[pr]]), _block_diag(k_t[j][:, sls[pr]])], axis=0))
             for j, pr in items]
        n_ab = [jnp.where(strict, gi[0:L, 0:W], 0.0) for gi in g]
        a_ak = [jnp.where(strict, gi[0:L, W:], 0.0) for gi in g]
        m_rb = [jnp.where(incl, gi[L:, 0:W], 0.0) for gi in g]
        m_rk = [jnp.where(incl, gi[L:, W:], 0.0) for gi in g]
        n_it = range(len(items))

        lm = level_mask(1)
        xinv = [eye + jnp.where(lm, n_ab[i], 0.0) for i in n_it]
        b = 2
        while b < L:
            lm = level_mask(b)
            t1 = [_dot(xinv[i], _block_diag(jnp.where(lm, n_ab[i], 0.0))) for i in n_it]
            q = [_dot(t1[i], _block_diag(xinv[i])) for i in n_it]
            xinv = [xinv[i] + q[i] for i in n_it]
            b *= 2

        v_bd = [_block_diag(vc[j][:, sls[pr]]) for j, pr in items]
        akv = [_dot(a_ak[i], v_bd[i]) for i in n_it]
        wu = [_dot(xinv[i], jnp.concatenate([_block_diag(a_t[j][:, sls[pr]]), _block_diag(akv[i])], axis=1))
              for i, (j, pr) in enumerate(items)]
        y_loc = [_dot(m_rk[i], v_bd[i]) for i in n_it]

        for j in range(RW_GROUP):
            it = [j * RW_SLABS + pr for pr in prs]
            s_bd = [st_ref[pr] for pr in prs]
            u_c = [_dot_nt(wu[it[pr]][:, 0:W], s_bd[pr]) + wu[it[pr]][:, W:] for pr in prs]
            y_r = [_dot_nt(r_t[j][:, sls[pr]], s_bd[pr]) for pr in prs]
            y_c = [y_r[pr] + _dot(m_rb[it[pr]], _block_diag(u_c[pr])) + y_loc[it[pr]] for pr in prs]
            upd = [_dot(jnp.concatenate([u_c[pr], vc[j][:, sls[pr]]], axis=0).T,
                        jnp.concatenate([b_h[j][:, sls[pr]], k_h[j][:, sls[pr]]], axis=0)) for pr in prs]
            for pr in prs:
                st_ref[pr] = s_bd[pr] * g_end[j][:, sls[pr]] + jnp.where(bd_mask, upd[pr], 0.0)
                y_s[rows[j], sls[pr]] = y_c[pr]
        return carry

    lax.fori_loop(0, tb // (L * RW_GROUP), group, 0)

    y = y_s[...]
    mean = _seg_sum(y, seg_ones) * (1.0 / HEAD)
    yc = y - mean
    var = _seg_sum(yc * yc, seg_ones, split=False) * (1.0 / HEAD)
    yn = yc * lax.rsqrt(var + RW_GN_EPS) * lng_ref[...] + lnb_ref[...]
    o_ref[...] = ((yn + bonus) * gate).astype(o_ref.dtype)


def _rwkv_call(x, ng, w, mu, w0, wl, a0, wg, k_k, k_a, r_k, ln_g, ln_b):
    B, S, _ = x.shape
    tb = RW_TB
    full = lambda shape: pl.BlockSpec(shape, lambda b, i: (0,) * len(shape))
    row = full((1, RW))
    return pl.pallas_call(
        _rwkv_kernel,
        grid=(B, S // tb),
        in_specs=[pl.BlockSpec((None, tb, D_MODEL), lambda b, i: (b, i, 0)),
                  full((1, D_MODEL)), full((D_MODEL, RW_COLS)), full((1, RW_COLS)), row,
                  full((LANES, 2 * RW)), row, full((GATE_LORA, RW)), row, row, row, row, row],
        out_specs=pl.BlockSpec((None, tb, RW), lambda b, i: (b, i, 0)),
        out_shape=jax.ShapeDtypeStruct((B, S, RW), BF16),
        scratch_shapes=[pltpu.VMEM((8, RW_COLS), F32), pltpu.VMEM((RW_SLABS, RW_SLAB, RW_SLAB), F32)]
        + [pltpu.VMEM((tb, RW), F32)] * 7,
        compiler_params=pltpu.CompilerParams(dimension_semantics=("arbitrary", "arbitrary"),
                                             vmem_limit_bytes=VMEM_LIMIT),
        name="rwkv",
    )(x, ng, w, mu, w0, wl, a0, wg, k_k, k_a, r_k, ln_g, ln_b)


def _ssd_kernel(x_ref, ng_ref, w_ref, cw_ref, cb_ref, dtb_ref, alog_ref, dx_ref, sng_ref, exp_ref, o_ref,
                prev_ref, st_ref, xs_s, b_s, c_s, acs_s, acst_s, dtt_s, y_s):
    tb = x_ref.shape[0]
    L = S_CHUNK

    @pl.when(pl.program_id(1) == 0)
    def _():
        prev_ref[...] = jnp.zeros_like(prev_ref)
        st_ref[...] = jnp.zeros_like(st_ref)

    h = _rms(x_ref[...], ng_ref[...])
    p = jnp.dot(h.astype(BF16), w_ref[...], preferred_element_type=F32)
    z = p[:, 0:SW]
    xbc = p[:, SW:SW + S_XBC]
    dt_raw = p[:, SW + S_XBC:]

    carry = prev_ref[...]
    prev_ref[...] = xbc[tb - 8:tb, :]
    row8 = _iota((8, S_XBC), 0)
    conv = cb_ref[...] + cw_ref[S_CONV - 1:S_CONV, :] * xbc
    for j in range(1, S_CONV):
        sh = pltpu.roll(xbc, j, 0)
        top = jnp.where(row8 < j, pltpu.roll(carry, j, 0), sh[0:8, :])
        sh = jnp.concatenate([top, sh[8:, :]], axis=0)
        conv = conv + cw_ref[S_CONV - 1 - j:S_CONV - j, :] * sh
    act = conv * _sigmoid(conv)
    xs = act[:, 0:SW]
    b_s[...] = act[:, SW:SW + S_GROUPS * S_STATE]
    c_s[...] = act[:, SW + S_GROUPS * S_STATE:]

    head_lane = _iota((tb, LANES), 1) < S_HEADS
    dt = jnp.where(head_lane, _softplus(dt_raw + dtb_ref[...]), 0.0)
    a_dt = dt * (-jnp.exp(alog_ref[...]))
    expand = exp_ref[...]
    xs_s[...] = xs

    ri = _iota((tb, tb), 0)
    ci = _iota((tb, tb), 1)
    same_chunk = (ri // L) == (ci // L)
    tril_blk = (same_chunk & (ci <= ri)).astype(BF16)
    triu_blk = (same_chunk & (ri <= ci)).astype(BF16)
    acs = _dot_sel_l(tril_blk, a_dt)
    acs_s[...] = acs
    acst = _dot_sel_r(a_dt.T, triu_blk)
    dtt = dt.T
    for c in range(tb // L):
        acst_s[c] = acst[:, c * L:(c + 1) * L]
        dtt_s[c] = dtt[:, c * L:(c + 1) * L]

    tril_ll = _iota((L, L), 1) <= _iota((L, L), 0)

    def chunk(c, carry_):
        rows = pl.ds(c * L, L)
        acs_c = acs_s[rows, :]
        acs_t = acst_s[c]
        dt_t = dtt_s[c]
        a_last = acs_c[L - 1:L, :]
        e_end = _dot_sel_r(jnp.broadcast_to(jnp.exp(a_last), (8, LANES)), expand)[0:1, :]
        xs_c = xs_s[rows, :]
        for g in range(S_GROUPS):
            gs = slice(g * S_STATE, (g + 1) * S_STATE)
            b_g = b_s[rows, gs]
            c_g = c_s[rows, gs]
            cb = _dot_nt(c_g, b_g)
            b_gt = b_g.T
            for q in range(PAIRS_PER_GROUP):
                pr = g * PAIRS_PER_GROUP + q
                sl = slice(pr * PAIR, (pr + 1) * PAIR)
                ms, cs_, bw = [], [], []
                for hh in (2 * pr, 2 * pr + 1):
                    col = jnp.broadcast_to(acs_c[:, hh:hh + 1], (L, L))
                    row = acs_t[hh:hh + 1, :] - jnp.log(dt_t[hh:hh + 1, :])
                    ms.append(cb * jnp.exp(jnp.where(tril_ll, col - row, -jnp.inf)))
                    cs_.append(c_g * jnp.exp(col))
                    bw.append(b_gt * jnp.exp(acs_t[hh:hh + 1, L - 1:L] - row))
                xs_bd = _block_diag(xs_c[:, sl])
                s_p = st_ref[pr]
                y_s[rows, sl] = _dot(jnp.concatenate(ms + cs_, axis=1),
                                     jnp.concatenate([xs_bd, _block_diag(s_p)], axis=0))
                st_ref[pr] = s_p * e_end[:, sl] + _dot(jnp.concatenate(bw, axis=1), xs_bd)
        return carry_

    for c in range(tb // L):
        chunk(c, 0)

    y = y_s[...] + xs * dx_ref[...]
    uu = y * (z * _sigmoid(z))
    gw = SW // S_GROUPS
    outs = []
    for g in range(S_GROUPS):
        ug = uu[:, g * gw:(g + 1) * gw]
        outs.append(ug * lax.rsqrt(jnp.mean(ug * ug, axis=-1, keepdims=True) + NORM_EPS))
    o_ref[...] = (jnp.concatenate(outs, axis=1) * sng_ref[...]).astype(o_ref.dtype)


def _ssd_call(x, ng, w, cw, cb, dtb, alog, dx, sng, expand):
    B, S, _ = x.shape
    tb = S_TB
    full = lambda shape: pl.BlockSpec(shape, lambda b, i: (0,) * len(shape))
    return pl.pallas_call(
        _ssd_kernel,
        grid=(B, S // tb),
        in_specs=[pl.BlockSpec((None, tb, D_MODEL), lambda b, i: (b, i, 0)),
                  full((1, D_MODEL)), full((D_MODEL, S_COLS_PAD)), full((S_CONV, S_XBC)), full((1, S_XBC)),
                  full((1, LANES)), full((1, LANES)), full((1, SW)), full((1, SW)), full((LANES, SW))],
        out_specs=pl.BlockSpec((None, tb, SW), lambda b, i: (b, i, 0)),
        out_shape=jax.ShapeDtypeStruct((B, S, SW), BF16),
        scratch_shapes=[pltpu.VMEM((8, S_XBC), F32), pltpu.VMEM((S_PAIRS, S_STATE, PAIR), F32),
                        pltpu.VMEM((tb, SW), F32),
                        pltpu.VMEM((tb, S_GROUPS * S_STATE), F32), pltpu.VMEM((tb, S_GROUPS * S_STATE), F32),
                        pltpu.VMEM((tb, LANES), F32), pltpu.VMEM((tb // S_CHUNK, LANES, S_CHUNK), F32),
                        pltpu.VMEM((tb // S_CHUNK, LANES, S_CHUNK), F32), pltpu.VMEM((tb, SW), F32)],
        compiler_params=pltpu.CompilerParams(dimension_semantics=("arbitrary", "arbitrary"),
                                             vmem_limit_bytes=VMEM_LIMIT),
        name="ssd",
    )(x, ng, w, cw, cb, dtb, alog, dx, sng, expand)


def _merge_kernel(x_ref, ya_ref, yb_ref, ng_ref, wgt_ref, bgt_ref, wa_ref, wb_ref, wo_ref, fg_ref,
                  wrh_ref, wrl_ref, br_ref, x1_ref, h2_ref, rt_ref, cnt_ref, carry_ref, *, tiles_per_moe):
    tm = x_ref.shape[0]
    i = pl.program_id(0)

    @pl.when(i % tiles_per_moe == 0)
    def _():
        carry_ref[...] = jnp.zeros_like(carry_ref)

    x = x_ref[...]
    h = _rms(x, ng_ref[...])
    gates = _sigmoid(jnp.dot(h.astype(BF16), wgt_ref[...], preferred_element_type=F32) + bgt_ref[...])
    up_a = jnp.dot(ya_ref[...], wa_ref[...], preferred_element_type=F32)
    up_b = jnp.dot(yb_ref[...], wb_ref[...], preferred_element_type=F32)
    merged = gates[:, :D_MODEL] * up_a + gates[:, D_MODEL:] * up_b
    x1 = x + _dot(merged, wo_ref[...])
    x1_ref[...] = x1
    h2 = _rms(x1, fg_ref[...])
    h2_ref[...] = h2

    hh, hl = _split(h2)
    wh, wl = wrh_ref[...], wrl_ref[...]
    logits = (jnp.dot(hh, wh, preferred_element_type=F32) + jnp.dot(hh, wl, preferred_element_type=F32)
              + jnp.dot(hl, wh, preferred_element_type=F32)) + br_ref[...]
    li = _iota(logits.shape, 1).astype(F32)
    neg = -jnp.inf
    big = float(LANES)

    gl = jnp.where(li < N_GROUPS, logits, neg)
    gmax = jnp.max(gl, axis=-1, keepdims=True)
    grp = jnp.min(jnp.where(gl == gmax, li, big), axis=-1, keepdims=True)
    g_w = 1.0 / jnp.sum(jnp.exp(gl - gmax), axis=-1, keepdims=True)

    lo_lane = ROUTE_E0 + EPG * grp
    el = jnp.where((li >= lo_lane) & (li < lo_lane + EPG), logits, neg)
    emax = jnp.max(el, axis=-1, keepdims=True)
    i1 = jnp.min(jnp.where(el == emax, li, big), axis=-1, keepdims=True)
    esum = jnp.sum(jnp.exp(el - emax), axis=-1, keepdims=True)
    el2 = jnp.where(li == i1, neg, el)
    m2 = jnp.max(el2, axis=-1, keepdims=True)
    i2 = jnp.min(jnp.where(el2 == m2, li, big), axis=-1, keepdims=True)
    p1 = 1.0 / esum
    p2 = jnp.exp(m2 - emax) / esum
    gate1 = g_w * p1 / (p1 + p2)
    gate2 = g_w * p2 / (p1 + p2)

    hit1 = li == i1
    hit2 = li == i2
    onehot = jnp.where(hit1 | hit2, 1.0, 0.0)
    tril_strict = (_iota((tm, tm), 1) < _iota((tm, tm), 0)).astype(BF16)
    cum = jnp.dot(tril_strict, onehot.astype(BF16), preferred_element_type=F32) + carry_ref[0:1, :]
    r1 = jnp.sum(jnp.where(hit1, cum, 0.0), axis=-1, keepdims=True)
    r2 = jnp.sum(jnp.where(hit2, cum, 0.0), axis=-1, keepdims=True)
    carry_ref[...] = carry_ref[...] + jnp.sum(onehot, axis=0, keepdims=True)
    cnt_ref[...] = carry_ref[...]

    e1 = i1 - ROUTE_E0
    e2 = i2 - ROUTE_E0
    rt = jnp.zeros(logits.shape, F32)
    for lane, val in enumerate((e1, e2, r1, r2, gate1, gate2)):
        rt = jnp.where(li == lane, val, rt)
    rt_ref[...] = rt


def _merge_call(x2, ya, yb, ng, wgt, bgt, wa, wb, wo, fg, wrh, wrl, br, moe_tile):
    T = x2.shape[0]
    tm = MERGE_TM
    tiles_per_moe = moe_tile // tm
    n_moe = T // moe_tile
    full = lambda shape: pl.BlockSpec(shape, lambda i: (0,) * len(shape))
    rows = lambda w: pl.BlockSpec((tm, w), lambda i: (i, 0))
    return pl.pallas_call(
        functools.partial(_merge_kernel, tiles_per_moe=tiles_per_moe),
        grid=(T // tm,),
        in_specs=[rows(D_MODEL), rows(RW), rows(SW), full((1, D_MODEL)), full((D_MODEL, 2 * D_MODEL)),
                  full((1, 2 * D_MODEL)), full((RW, D_MODEL)), full((SW, D_MODEL)), full((D_MODEL, D_MODEL)),
                  full((1, D_MODEL)), full((D_MODEL, LANES)), full((D_MODEL, LANES)), full((1, LANES))],
        out_specs=[rows(D_MODEL), rows(D_MODEL), rows(LANES),
                   pl.BlockSpec((None, 8, LANES), lambda i: (i // tiles_per_moe, 0, 0))],
        out_shape=[jax.ShapeDtypeStruct((T, D_MODEL), F32), jax.ShapeDtypeStruct((T, D_MODEL), F32),
                   jax.ShapeDtypeStruct((T, LANES), F32), jax.ShapeDtypeStruct((n_moe, 8, LANES), F32)],
        scratch_shapes=[pltpu.VMEM((8, LANES), F32)],
        compiler_params=pltpu.CompilerParams(dimension_semantics=("arbitrary",),
                                             vmem_limit_bytes=VMEM_LIMIT),
        name="merge",
    )(x2, ya, yb, ng, wgt, bgt, wa, wb, wo, fg, wrh, wrl, br)


def _moe_kernel(idx_ref, gts_ref, cnt_ref, h2_ref, wg_ref, wu_ref, wd_ref, y_ref,
                start_ref, stok_ref, sg_ref, xb_ref, ob_ref):
    e = pl.program_id(1)
    tt = h2_ref.shape[0]

    @pl.when(e == 0)
    def _():
        def starts(j, acc):
            start_ref[j] = acc
            return acc + cnt_ref[0, j]
        lax.fori_loop(0, N_EXP, starts, jnp.int32(0))

        def place(j, c):
            for u in range(MOE_UNROLL):
                t = j * MOE_UNROLL + u
                for k in range(2):
                    slot = start_ref[idx_ref[0, k * tt + t]] + idx_ref[0, (2 + k) * tt + t]
                    stok_ref[slot] = t
                    sg_ref[slot] = gts_ref[0, k * tt + t]
            return c
        lax.fori_loop(0, tt // MOE_UNROLL, place, 0)
        y_ref[...] = jnp.zeros_like(y_ref)
        xb_ref[...] = jnp.zeros_like(xb_ref)

    n = cnt_ref[0, e]
    s0 = start_ref[e]
    last_slot = 2 * tt - 1

    def block(b, c):
        base = s0 + b * MOE_ROWS
        m = jnp.minimum(n - b * MOE_ROWS, MOE_ROWS)
        groups = (m + MOE_UNROLL - 1) // MOE_UNROLL

        def gather(j, c2):
            dst = xb_ref.at[pl.ds(pl.multiple_of(j * MOE_UNROLL, MOE_UNROLL), MOE_UNROLL)]
            for u in range(MOE_UNROLL):
                t = stok_ref[jnp.minimum(base + j * MOE_UNROLL + u, last_slot)]
                dst[u:u + 1, :] = h2_ref[pl.ds(t, 1), :]
            return c2
        lax.fori_loop(0, groups, gather, 0)

        xb = xb_ref[...].astype(BF16)
        hg = jnp.dot(xb, wg_ref[...], preferred_element_type=F32)
        hu = jnp.dot(xb, wu_ref[...], preferred_element_type=F32)
        hid = hg * _sigmoid(hg) * hu
        ob_ref[...] = jnp.dot(hid.astype(BF16), wd_ref[...], preferred_element_type=F32)

        def scatter(j, c2):
            done = []
            src = ob_ref.at[pl.ds(pl.multiple_of(j * MOE_UNROLL, MOE_UNROLL), MOE_UNROLL)]
            for u in range(MOE_UNROLL):
                slot = base + j * MOE_UNROLL + u
                t = stok_ref[slot]
                done.append((t, y_ref[pl.ds(t, 1), :] + sg_ref[slot] * src[u:u + 1, :]))
            for t, val in done:
                y_ref[pl.ds(t, 1), :] = val
            return c2
        full = m // MOE_UNROLL
        lax.fori_loop(0, full, scatter, 0)

        def scatter_row(r, c2):
            t = stok_ref[base + r]
            y_ref[pl.ds(t, 1), :] = y_ref[pl.ds(t, 1), :] + sg_ref[base + r] * ob_ref[pl.ds(r, 1), :]
            return c2
        lax.fori_loop(full * MOE_UNROLL, m, scatter_row, 0)
        return c

    lax.fori_loop(0, (n + MOE_ROWS - 1) // MOE_ROWS, block, 0)


def _moe_call(idx, gts, cnt, h2, wg, wu, wd, moe_tile):
    T = h2.shape[0]
    tt = moe_tile
    smem = lambda w: pl.BlockSpec((None, 1, w), lambda i, e: (i, 0, 0), memory_space=pltpu.SMEM)
    return pl.pallas_call(
        _moe_kernel,
        grid=(T // tt, N_EXP),
        in_specs=[smem(4 * tt), smem(2 * tt), smem(N_EXP),
                  pl.BlockSpec((tt, D_MODEL), lambda i, e: (i, 0), pipeline_mode=pl.Buffered(1)),
                  pl.BlockSpec((None, D_MODEL, D_EXP), lambda i, e: (e, 0, 0)),
                  pl.BlockSpec((None, D_MODEL, D_EXP), lambda i, e: (e, 0, 0)),
                  pl.BlockSpec((None, D_EXP, D_MODEL), lambda i, e: (e, 0, 0))],
        out_specs=pl.BlockSpec((tt, D_MODEL), lambda i, e: (i, 0), pipeline_mode=pl.Buffered(1)),
        out_shape=jax.ShapeDtypeStruct((T, D_MODEL), F32),
        scratch_shapes=[pltpu.SMEM((N_EXP,), jnp.int32), pltpu.SMEM((2 * tt,), jnp.int32),
                        pltpu.SMEM((2 * tt,), F32), pltpu.VMEM((MOE_ROWS, D_MODEL), F32),
                        pltpu.VMEM((MOE_ROWS, D_MODEL), F32)],
        compiler_params=pltpu.CompilerParams(dimension_semantics=("arbitrary", "arbitrary"),
                                             vmem_limit_bytes=VMEM_LIMIT),
        name="moe",
    )(idx, gts, cnt, h2, wg, wu, wd)


def _final_kernel(x1_ref, y_ref, g_ref, o_ref):
    o_ref[...] = _rms(x1_ref[...] + y_ref[...], g_ref[...])


def _final_call(x1, y, g):
    T = x1.shape[0]
    tm = 512
    rows = pl.BlockSpec((tm, D_MODEL), lambda i: (i, 0))
    return pl.pallas_call(
        _final_kernel,
        grid=(T // tm,),
        in_specs=[rows, rows, pl.BlockSpec((1, D_MODEL), lambda i: (0, 0))],
        out_specs=rows,
        out_shape=jax.ShapeDtypeStruct((T, D_MODEL), F32),
        compiler_params=pltpu.CompilerParams(dimension_semantics=("arbitrary",)),
        name="final",
    )(x1, y, g)


def _layer(x, attn_norm_g, w_in, b_gate, rwkv_mu, rwkv_w0, rwkv_w_decay, rwkv_a0, rwkv_w_a, rwkv_w_g,
           rwkv_k_k, rwkv_k_a, rwkv_r_k, rwkv_ln_g, rwkv_ln_b, w_up_rwkv, ssd_conv_w, ssd_conv_b,
           ssd_dt_bias, ssd_a_log, ssd_d, ssd_norm_g, w_up_ssd, w_out, ffn_norm_g, w_router_group,
           b_router_group, w_router_expert, b_router_expert, w_exp_gate, w_exp_up, w_exp_down):
    B, S, _ = x.shape
    T = B * S
    row = lambda a: a.reshape(1, -1).astype(F32)
    ng = row(attn_norm_g)

    w_rw = w_in[:, :RW_COLS].astype(BF16)
    wl = jnp.zeros((LANES, 2 * RW), F32)
    wl = wl.at[:DECAY_LORA, :RW].set(rwkv_w_decay).at[DECAY_LORA:, RW:].set(rwkv_w_a).astype(BF16)
    y_a = _rwkv_call(x, ng, w_rw, row(rwkv_mu), row(rwkv_w0), wl, row(rwkv_a0), rwkv_w_g.astype(BF16),
                     row(rwkv_k_k), row(rwkv_k_a), row(rwkv_r_k), row(rwkv_ln_g), row(rwkv_ln_b))

    n_ssd = 2 * SW + 2 * S_GROUPS * S_STATE + S_HEADS
    w_ssd = jnp.pad(w_in[:, RW_COLS:RW_COLS + n_ssd], ((0, 0), (0, S_COLS_PAD - n_ssd))).astype(BF16)
    pad_h = lambda a: jnp.pad(a.reshape(1, -1).astype(F32), ((0, 0), (0, LANES - S_HEADS)))
    expand = (jnp.arange(LANES)[:, None] == (jnp.arange(SW) // HEAD)[None, :]).astype(BF16)
    y_b = _ssd_call(x, ng, w_ssd, ssd_conv_w.astype(F32), row(ssd_conv_b), pad_h(ssd_dt_bias), pad_h(ssd_a_log),
                    row(jnp.repeat(ssd_d, HEAD)), row(ssd_norm_g), expand)

    moe_tile = min(MOE_TILE, T)
    w_gt = w_in[:, RW_COLS + n_ssd:].astype(BF16)
    w_r = jnp.zeros((D_MODEL, LANES), F32)
    w_r = w_r.at[:, :N_GROUPS].set(w_router_group).at[:, ROUTE_E0:ROUTE_E0 + N_EXP].set(w_router_expert)
    w_rh = w_r.astype(BF16)
    w_rl = (w_r - w_rh.astype(F32)).astype(BF16)
    b_r = jnp.zeros((1, LANES), F32)
    b_r = b_r.at[0, :N_GROUPS].set(b_router_group).at[0, ROUTE_E0:ROUTE_E0 + N_EXP].set(b_router_expert)
    x1, h2, route, cnt = _merge_call(
        x.reshape(T, D_MODEL), y_a.reshape(T, RW), y_b.reshape(T, SW), ng, w_gt, row(b_gate),
        w_up_rwkv.astype(BF16), w_up_ssd.astype(BF16), w_out.astype(BF16), row(ffn_norm_g), w_rh, w_rl, b_r,
        moe_tile)

    n_moe = T // moe_tile
    idx = route[:, 0:4].astype(jnp.int32).reshape(n_moe, moe_tile, 4).transpose(0, 2, 1).reshape(n_moe, 1, 4 * moe_tile)
    gts = route[:, 4:6].reshape(n_moe, moe_tile, 2).transpose(0, 2, 1).reshape(n_moe, 1, 2 * moe_tile)
    counts = cnt[:, 0:1, ROUTE_E0:ROUTE_E0 + N_EXP].astype(jnp.int32)
    y_moe = _moe_call(idx, gts, counts, h2, w_exp_gate.astype(BF16), w_exp_up.astype(BF16),
                      w_exp_down.astype(BF16), moe_tile)
    return x1, y_moe


def kernel(x, attn_norm_g, w_in, b_gate, rwkv_mu, rwkv_w0, rwkv_w_decay, rwkv_a0, rwkv_w_a, rwkv_w_g, rwkv_k_k, rwkv_k_a, rwkv_r_k, rwkv_ln_g, rwkv_ln_b, w_up_rwkv, ssd_conv_w, ssd_conv_b, ssd_dt_bias, ssd_a_log, ssd_d, ssd_norm_g, w_up_ssd, w_out, ffn_norm_g, w_router_group, b_router_group, w_router_expert, b_router_expert, w_exp_gate, w_exp_up, w_exp_down, final_norm_g):
    B, S, _ = x.shape
    depth = attn_norm_g.shape[0]
    assert depth == 1, "the final residual add is fused with the last layer's MoE output"
    layer_params = (attn_norm_g, w_in, b_gate, rwkv_mu, rwkv_w0, rwkv_w_decay, rwkv_a0, rwkv_w_a, rwkv_w_g,
                    rwkv_k_k, rwkv_k_a, rwkv_r_k, rwkv_ln_g, rwkv_ln_b, w_up_rwkv, ssd_conv_w, ssd_conv_b,
                    ssd_dt_bias, ssd_a_log, ssd_d, ssd_norm_g, w_up_ssd, w_out, ffn_norm_g, w_router_group,
                    b_router_group, w_router_expert, b_router_expert, w_exp_gate, w_exp_up, w_exp_down)
    x1, y_moe = _layer(x, *(prm[0] for prm in layer_params))
    out = _final_call(x1, y_moe, final_norm_g.reshape(1, -1).astype(F32))
    return out.reshape(B, S, D_MODEL)
```

```python
import functools

import jax
import jax.numpy as jnp
from jax import lax
from jax.experimental import pallas as pl
from jax.experimental.pallas import tpu as pltpu

F32 = jnp.float32
BF16 = jnp.bfloat16

D_MODEL = 1024
NORM_EPS = 1e-6
LANES = 128
HEAD = 64
PAIR = 2 * HEAD

RW = 1024
RW_SLAB = PAIR
RW_SLABS = RW // RW_SLAB
DECAY_LORA = 64
AAA_LORA = 64
GATE_LORA = 128
RW_COLS = 3 * RW + DECAY_LORA + AAA_LORA + GATE_LORA
RW_GN_EPS = 64e-5
RW_CHUNK = 64
RW_GROUP = 4
RW_TB = 256
EXP_NEG_HALF = 0.6065306597126334

SW = 2048
S_HEADS = SW // HEAD
S_PAIRS = SW // PAIR
S_GROUPS = 4
S_STATE = 128
S_CONV = 4
S_XBC = SW + 2 * S_GROUPS * S_STATE
S_COLS_PAD = SW + S_XBC + LANES
S_CHUNK = 128
S_TB = 256
PAIRS_PER_GROUP = S_PAIRS // S_GROUPS

N_GROUPS = 4
EPG = 8
N_EXP = 32
D_EXP = 512
MOE_ROWS = 320
MOE_TILE = 4096
MOE_UNROLL = 16
MERGE_TM = 512
ROUTE_E0 = N_GROUPS

VMEM_LIMIT = 56 * 1024 * 1024


def _dot(a, b):
    return jnp.dot(a.astype(BF16), b.astype(BF16), preferred_element_type=F32)


def _dot_nt(a, b):
    return lax.dot_general(a.astype(BF16), b.astype(BF16), (((1,), (1,)), ((), ())),
                           preferred_element_type=F32)


def _split(x):
    hi = x.astype(BF16)
    lo = (x - hi.astype(F32)).astype(BF16)
    return hi, lo


def _dot_sel_l(sel, x):
    hi, lo = _split(x)
    return (jnp.dot(sel, hi, preferred_element_type=F32)
            + jnp.dot(sel, lo, preferred_element_type=F32))


def _dot_sel_r(x, sel):
    hi, lo = _split(x)
    return (jnp.dot(hi, sel, preferred_element_type=F32)
            + jnp.dot(lo, sel, preferred_element_type=F32))


def _iota(shape, dim):
    return lax.broadcasted_iota(jnp.int32, shape, dim)


def _softplus(x):
    return jnp.maximum(x, 0.0) + jnp.log(1.0 + jnp.exp(-jnp.abs(x)))


def _sigmoid(x):
    return 1.0 / (1.0 + jnp.exp(-x))


def _rms(x, g):
    return x * lax.rsqrt(jnp.mean(x * x, axis=-1, keepdims=True) + NORM_EPS) * g


def _block_diag(y):
    head = _iota(y.shape, 1) // HEAD
    yb = y.astype(BF16)
    zero = jnp.zeros_like(yb)
    return jnp.concatenate([jnp.where(head == i, yb, zero) for i in range(y.shape[1] // HEAD)], axis=0)


def _seg_sum(x, seg_ones, split=True):
    outs = []
    w = seg_ones.shape[0]
    for c in range(x.shape[1] // w):
        xc = x[:, c * w:(c + 1) * w]
        outs.append(_dot_sel_r(xc, seg_ones) if split else _dot(xc, seg_ones))
    return jnp.concatenate(outs, axis=1)


def _rwkv_kernel(x_ref, ng_ref, w_ref, mu_ref, w0_ref, wl_ref, a0_ref, wg_ref, kk_ref, ka_ref, rk_ref,
                 lng_ref, lnb_ref, o_ref,
                 prev_ref, st_ref, r_s, k_s, kk_s, b_s, v_s, ld_s, y_s):
    tb = x_ref.shape[0]
    L = RW_CHUNK

    h = _rms(x_ref[...], ng_ref[...])
    p = jnp.dot(h.astype(BF16), w_ref[...], preferred_element_type=F32)

    rolled = pltpu.roll(p, 1, 0)
    top = jnp.where(_iota((8, RW_COLS), 0) == 0, prev_ref[7:8, :], rolled[0:8, :])
    shifted = jnp.concatenate([top, rolled[8:, :]], axis=0)
    prev_ref[...] = p[tb - 8:tb, :]
    u = p + mu_ref[...] * (shifted - p)

    r = u[:, 0:RW]
    k = u[:, RW:2 * RW]
    v = u[:, 2 * RW:3 * RW]
    zwa = u[:, 3 * RW:3 * RW + LANES]
    zg = u[:, 3 * RW + LANES:3 * RW + 2 * LANES]
    lora_in = jnp.where(_iota(zwa.shape, 1) < DECAY_LORA, jnp.tanh(zwa), zwa)
    dl = _dot(lora_in, wl_ref[...])
    ld = -EXP_NEG_HALF * _sigmoid(w0_ref[...] + dl[:, :RW])
    a = _sigmoid(a0_ref[...] + dl[:, RW:])
    gate = _dot(_sigmoid(zg), wg_ref[...])

    seg_ones = (_iota((2 * PAIR, 2 * PAIR), 0) // HEAD == _iota((2 * PAIR, 2 * PAIR), 1) // HEAD).astype(BF16)
    kk = k * kk_ref[...]
    kk = kk * lax.rsqrt(jnp.maximum(_seg_sum(kk * kk, seg_ones, split=False), 1e-24))
    k2 = k * (1.0 + (a - 1.0) * ka_ref[...])
    bonus = _seg_sum(r * k2 * rk_ref[...], seg_ones) * v

    r_s[...] = r
    k_s[...] = k2
    kk_s[...] = kk
    b_s[...] = kk * a
    v_s[...] = v
    ld_s[...] = ld

    W = RW_SLAB
    t_i = _iota((L, W), 0)
    s_i = _iota((L, W), 1) % HEAD
    strict = s_i < t_i
    incl = s_i <= t_i
    eye = (s_i == t_i).astype(F32)
    tril_l = (_iota((L, L), 1) <= _iota((L, L), 0)).astype(BF16)
    bd_mask = (_iota((W, W), 0) // HEAD) == (_iota((W, W), 1) // HEAD)

    def level_mask(b):
        same = (t_i // (2 * b)) == (s_i // (2 * b))
        return same & ((t_i // b) % 2 == 1) & ((s_i // b) % 2 == 0)

    prs = range(RW_SLABS)
    sls = [slice(pr * W, (pr + 1) * W) for pr in prs]

    def group(cg, carry):
        rows, a_t, r_t, b_t, k_t, b_h, k_h, vc, g_end = [], [], [], [], [], [], [], [], []
        for j in range(RW_GROUP):
            rj = pl.ds(pl.multiple_of((cg * RW_GROUP + j) * L, L), L)
            ldc = ld_s[rj, :]
            cs = _dot_sel_l(tril_l, ldc)
            cs_last = cs[L - 1:L, :]
            e_inv = jnp.exp(-cs)
            e_end = jnp.exp(cs_last - cs)
            bc = b_s[rj, :]
            kc = k_s[rj, :]
            rows.append(rj)
            a_t.append(-kk_s[rj, :] * jnp.exp(cs - ldc))
            r_t.append(r_s[rj, :] * jnp.exp(cs))
            b_t.append(bc * e_inv)
            k_t.append(kc * e_inv)
            b_h.append(bc * e_end)
            k_h.append(kc * e_end)
            vc.append(v_s[rj, :])
            g_end.append(jnp.exp(cs_last))

        items = [(j, pr) for j in range(RW_GROUP) for pr in prs]
        g = [_dot_nt(jnp.concatenate([a_t[j][:, sls[pr]], r_t[j][:, sls[pr]]], axis=0),
                     jnp.concatenate([_block_diag(b_t[j][:, sls[pr]]), _block_diag(k_t[j][:, sls[pr]])], axis=0))
             for j, pr in items]
        n_ab = [jnp.where(strict, gi[0:L, 0:W], 0.0) for gi in g]
        a_ak = [jnp.where(strict, gi[0:L, W:], 0.0) for gi in g]
        m_rb = [jnp.where(incl, gi[L:, 0:W], 0.0) for gi in g]
        m_rk = [jnp.where(incl, gi[L:, W:], 0.0) for gi in g]
        n_it = range(len(items))

        lm = level_mask(1)
        xinv = [eye + jnp.where(lm, n_ab[i], 0.0) for i in n_it]
        b = 2
        while b < L:
            lm = level_mask(b)
            t1 = [_dot(xinv[i], _block_diag(jnp.where(lm, n_ab[i], 0.0))) for i in n_it]
            q = [_dot(t1[i], _block_diag(xinv[i])) for i in n_it]
            xinv = [xinv[i] + q[i] for i in n_it]
            b *= 2

        v_bd = [_block_diag(vc[j][:, sls[pr]]) for j, pr in items]
        akv = [_dot(a_ak[i], v_bd[i]) for i in n_it]
        wu = [_dot(xinv[i], jnp.concatenate([_block_diag(a_t[j][:, sls[pr]]), _block_diag(akv[i])], axis=1))
              for i, (j, pr) in enumerate(items)]
        y_loc = [_dot(m_rk[i], v_bd[i]) for i in n_it]

        for j in range(RW_GROUP):
            it = [j * RW_SLABS + pr for pr in prs]
            s_bd = [st_ref[pr] for pr in prs]
            u_c = [_dot_nt(wu[it[pr]][:, 0:W], s_bd[pr]) + wu[it[pr]][:, W:] for pr in prs]
            y_r = [_dot_nt(r_t[j][:, sls[pr]], s_bd[pr]) for pr in prs]
            y_c = [y_r[pr] + _dot(m_rb[it[pr]], _block_diag(u_c[pr])) + y_loc[it[pr]] for pr in prs]
            upd = [_dot(jnp.concatenate([u_c[pr], vc[j][:, sls[pr]]], axis=0).T,
                        jnp.concatenate([b_h[j][:, sls[pr]], k_h[j][:, sls[pr]]], axis=0)) for pr in prs]
            for pr in prs:
                st_ref[pr] = s_bd[pr] * g_end[j][:, sls[pr]] + jnp.where(bd_mask, upd[pr], 0.0)
                y_s[rows[j], sls[pr]] = y_c[pr]
        return carry

    lax.fori_loop(0, tb // (L * RW_GROUP), group, 0)

    y = y_s[...]
    mean = _seg_sum(y, seg_ones) * (1.0 / HEAD)
    yc = y - mean
    var = _seg_sum(yc * yc, seg_ones, split=False) * (1.0 / HEAD)
    yn = yc * lax.rsqrt(var + RW_GN_EPS) * lng_ref[...] + lnb_ref[...]
    o_ref[...] = ((yn + bonus) * gate).astype(o_ref.dtype)


N_RW_IN, N_RW_SCRATCH = 11, 9
N_SSD_IN, N_SSD_SCRATCH = 8, 9


def _mix_kernel(*refs):
    x_ref, ng_ref = refs[0:2]
    rw_in = refs[2:2 + N_RW_IN]
    ssd_in = refs[2 + N_RW_IN:2 + N_RW_IN + N_SSD_IN]
    oa_ref, ob_ref = refs[2 + N_RW_IN + N_SSD_IN:4 + N_RW_IN + N_SSD_IN]
    scratch = refs[4 + N_RW_IN + N_SSD_IN:]
    rw_scr, ssd_scr = scratch[:N_RW_SCRATCH], scratch[N_RW_SCRATCH:]

    @pl.when(pl.program_id(1) == 0)
    def _():
        for carry in (rw_scr[0], rw_scr[1], ssd_scr[0], ssd_scr[1]):
            carry[...] = jnp.zeros_like(carry)

    _rwkv_kernel(x_ref, ng_ref, *rw_in, oa_ref, *rw_scr)
    _ssd_kernel(x_ref, ng_ref, *ssd_in, ob_ref, *ssd_scr)


def _mix_call(x, ng, rw_args, ssd_args):
    B, S, _ = x.shape
    tb = RW_TB
    assert S_TB == RW_TB
    full = lambda shape: pl.BlockSpec(shape, lambda b, i: (0,) * len(shape), pipeline_mode=pl.Buffered(1))
    row = full((1, RW))
    blk = lambda w: pl.BlockSpec((None, tb, w), lambda b, i: (b, i, 0))
    return pl.pallas_call(
        _mix_kernel,
        grid=(B, S // tb),
        in_specs=[blk(D_MODEL), full((1, D_MODEL)),
                  full((D_MODEL, RW_COLS)), full((1, RW_COLS)), row, full((LANES, 2 * RW)), row,
                  full((GATE_LORA, RW)), row, row, row, row, row,
                  full((D_MODEL, S_COLS_PAD)), full((S_CONV, S_XBC)), full((1, S_XBC)),
                  full((1, LANES)), full((1, LANES)), full((1, SW)), full((1, SW)), full((LANES, SW))],
        out_specs=[blk(RW), blk(SW)],
        out_shape=[jax.ShapeDtypeStruct((B, S, RW), BF16), jax.ShapeDtypeStruct((B, S, SW), BF16)],
        scratch_shapes=[pltpu.VMEM((8, RW_COLS), F32), pltpu.VMEM((RW_SLABS, RW_SLAB, RW_SLAB), F32)]
        + [pltpu.VMEM((tb, RW), F32)] * 7
        + [pltpu.VMEM((8, S_XBC), F32), pltpu.VMEM((S_PAIRS, S_STATE, PAIR), F32),
           pltpu.VMEM((tb, SW), F32),
           pltpu.VMEM((tb, S_GROUPS * S_STATE), F32), pltpu.VMEM((tb, S_GROUPS * S_STATE), F32),
           pltpu.VMEM((tb, LANES), F32), pltpu.VMEM((tb // S_CHUNK, LANES, S_CHUNK), F32),
           pltpu.VMEM((tb // S_CHUNK, LANES, S_CHUNK), F32), pltpu.VMEM((tb, SW), F32)],
        compiler_params=pltpu.CompilerParams(dimension_semantics=("arbitrary", "arbitrary"),
                                             vmem_limit_bytes=VMEM_LIMIT),
        name="mix",
    )(x, ng, *rw_args, *ssd_args)


def _ssd_kernel(x_ref, ng_ref, w_ref, cw_ref, cb_ref, dtb_ref, alog_ref, dx_ref, sng_ref, exp_ref, o_ref,
                prev_ref, st_ref, xs_s, b_s, c_s, acs_s, acst_s, dtt_s, y_s):
    tb = x_ref.shape[0]
    L = S_CHUNK

    h = _rms(x_ref[...], ng_ref[...])
    p = jnp.dot(h.astype(BF16), w_ref[...], preferred_element_type=F32)
    z = p[:, 0:SW]
    xbc = p[:, SW:SW + S_XBC]
    dt_raw = p[:, SW + S_XBC:]

    carry = prev_ref[...]
    prev_ref[...] = xbc[tb - 8:tb, :]
    row8 = _iota((8, S_XBC), 0)
    conv = cb_ref[...] + cw_ref[S_CONV - 1:S_CONV, :] * xbc
    for j in range(1, S_CONV):
        sh = pltpu.roll(xbc, j, 0)
        top = jnp.where(row8 < j, pltpu.roll(carry, j, 0), sh[0:8, :])
        sh = jnp.concatenate([top, sh[8:, :]], axis=0)
        conv = conv + cw_ref[S_CONV - 1 - j:S_CONV - j, :] * sh
    act = conv * _sigmoid(conv)
    xs = act[:, 0:SW]
    b_s[...] = act[:, SW:SW + S_GROUPS * S_STATE]
    c_s[...] = act[:, SW + S_GROUPS * S_STATE:]

    head_lane = _iota((tb, LANES), 1) < S_HEADS
    dt = jnp.where(head_lane, _softplus(dt_raw + dtb_ref[...]), 0.0)
    a_dt = dt * (-jnp.exp(alog_ref[...]))
    expand = exp_ref[...]
    xs_s[...] = xs

    ri = _iota((tb, tb), 0)
    ci = _iota((tb, tb), 1)
    same_chunk = (ri // L) == (ci // L)
    tril_blk = (same_chunk & (ci <= ri)).astype(BF16)
    triu_blk = (same_chunk & (ri <= ci)).astype(BF16)
    acs = _dot_sel_l(tril_blk, a_dt)
    acs_s[...] = acs
    acst = _dot_sel_r(a_dt.T, triu_blk)
    dtt = dt.T
    for c in range(tb // L):
        acst_s[c] = acst[:, c * L:(c + 1) * L]
        dtt_s[c] = dtt[:, c * L:(c + 1) * L]

    tril_ll = _iota((L, L), 1) <= _iota((L, L), 0)

    def chunk(c, carry_):
        rows = pl.ds(c * L, L)
        acs_c = acs_s[rows, :]
        acs_t = acst_s[c]
        dt_t = dtt_s[c]
        a_last = acs_c[L - 1:L, :]
        e_end = _dot_sel_r(jnp.broadcast_to(jnp.exp(a_last), (8, LANES)), expand)[0:1, :]
        xs_c = xs_s[rows, :]
        for g in range(S_GROUPS):
            gs = slice(g * S_STATE, (g + 1) * S_STATE)
            b_g = b_s[rows, gs]
            c_g = c_s[rows, gs]
            cb = _dot_nt(c_g, b_g)
            b_gt = b_g.T
            for q in range(PAIRS_PER_GROUP):
                pr = g * PAIRS_PER_GROUP + q
                sl = slice(pr * PAIR, (pr + 1) * PAIR)
                ms, cs_, bw = [], [], []
                for hh in (2 * pr, 2 * pr + 1):
                    col = jnp.broadcast_to(acs_c[:, hh:hh + 1], (L, L))
                    row = acs_t[hh:hh + 1, :] - jnp.log(dt_t[hh:hh + 1, :])
                    ms.append(cb * jnp.exp(jnp.where(tril_ll, col - row, -jnp.inf)))
                    cs_.append(c_g * jnp.exp(col))
                    bw.append(b_gt * jnp.exp(acs_t[hh:hh + 1, L - 1:L] - row))
                xs_bd = _block_diag(xs_c[:, sl])
                s_p = st_ref[pr]
                y_s[rows, sl] = _dot(jnp.concatenate(ms + cs_, axis=1),
                                     jnp.concatenate([xs_bd, _block_diag(s_p)], axis=0))
                st_ref[pr] = s_p * e_end[:, sl] + _dot(jnp.concatenate(bw, axis=1), xs_bd)
        return carry_

    for c in range(tb // L):
        chunk(c, 0)

    y = y_s[...] + xs * dx_ref[...]
    uu = y * (z * _sigmoid(z))
    gw = SW // S_GROUPS
    outs = []
    for g in range(S_GROUPS):
        ug = uu[:, g * gw:(g + 1) * gw]
        outs.append(ug * lax.rsqrt(jnp.mean(ug * ug, axis=-1, keepdims=True) + NORM_EPS))
    o_ref[...] = (jnp.concatenate(outs, axis=1) * sng_ref[...]).astype(o_ref.dtype)


def _merge_kernel(x_ref, ya_ref, yb_ref, ng_ref, wgt_ref, bgt_ref, wa_ref, wb_ref, wo_ref, fg_ref,
                  wrh_ref, wrl_ref, br_ref, x1_ref, h2_ref, rt_ref, cnt_ref, carry_ref, *, tiles_per_moe):
    tm = x_ref.shape[0]
    i = pl.program_id(0)

    @pl.when(i % tiles_per_moe == 0)
    def _():
        carry_ref[...] = jnp.zeros_like(carry_ref)

    x = x_ref[...]
    h = _rms(x, ng_ref[...])
    gates = _sigmoid(jnp.dot(h.astype(BF16), wgt_ref[...], preferred_element_type=F32) + bgt_ref[...])
    up_a = jnp.dot(ya_ref[...], wa_ref[...], preferred_element_type=F32)
    up_b = jnp.dot(yb_ref[...], wb_ref[...], preferred_element_type=F32)
    merged = gates[:, :D_MODEL] * up_a + gates[:, D_MODEL:] * up_b
    x1 = x + _dot(merged, wo_ref[...])
    x1_ref[...] = x1
    h2 = _rms(x1, fg_ref[...])
    h2_ref[...] = h2

    hh, hl = _split(h2)
    wh, wl = wrh_ref[...], wrl_ref[...]
    logits = (jnp.dot(hh, wh, preferred_element_type=F32) + jnp.dot(hh, wl, preferred_element_type=F32)
              + jnp.dot(hl, wh, preferred_element_type=F32)) + br_ref[...]
    li = _iota(logits.shape, 1).astype(F32)
    neg = -jnp.inf
    big = float(LANES)

    gl = jnp.where(li < N_GROUPS, logits, neg)
    gmax = jnp.max(gl, axis=-1, keepdims=True)
    grp = jnp.min(jnp.where(gl == gmax, li, big), axis=-1, keepdims=True)
    g_w = 1.0 / jnp.sum(jnp.exp(gl - gmax), axis=-1, keepdims=True)

    lo_lane = ROUTE_E0 + EPG * grp
    el = jnp.where((li >= lo_lane) & (li < lo_lane + EPG), logits, neg)
    emax = jnp.max(el, axis=-1, keepdims=True)
    i1 = jnp.min(jnp.where(el == emax, li, big), axis=-1, keepdims=True)
    esum = jnp.sum(jnp.exp(el - emax), axis=-1, keepdims=True)
    el2 = jnp.where(li == i1, neg, el)
    m2 = jnp.max(el2, axis=-1, keepdims=True)
    i2 = jnp.min(jnp.where(el2 == m2, li, big), axis=-1, keepdims=True)
    p1 = 1.0 / esum
    p2 = jnp.exp(m2 - emax) / esum
    gate1 = g_w * p1 / (p1 + p2)
    gate2 = g_w * p2 / (p1 + p2)

    hit1 = li == i1
    hit2 = li == i2
    onehot = jnp.where(hit1 | hit2, 1.0, 0.0)
    tril_strict = (_iota((tm, tm), 1) < _iota((tm, tm), 0)).astype(BF16)
    cum = jnp.dot(tril_strict, onehot.astype(BF16), preferred_element_type=F32) + carry_ref[0:1, :]
    r1 = jnp.sum(jnp.where(hit1, cum, 0.0), axis=-1, keepdims=True)
    r2 = jnp.sum(jnp.where(hit2, cum, 0.0), axis=-1, keepdims=True)
    carry_ref[...] = carry_ref[...] + jnp.sum(onehot, axis=0, keepdims=True)
    cnt_ref[...] = carry_ref[...]

    e1 = i1 - ROUTE_E0
    e2 = i2 - ROUTE_E0
    rt = jnp.zeros(logits.shape, F32)
    for lane, val in enumerate((e1, e2, r1, r2, gate1, gate2)):
        rt = jnp.where(li == lane, val, rt)
    rt_ref[...] = rt


def _merge_call(x2, ya, yb, ng, wgt, bgt, wa, wb, wo, fg, wrh, wrl, br, moe_tile):
    T = x2.shape[0]
    tm = MERGE_TM
    tiles_per_moe = moe_tile // tm
    n_moe = T // moe_tile
    full = lambda shape: pl.BlockSpec(shape, lambda i: (0,) * len(shape))
    rows = lambda w: pl.BlockSpec((tm, w), lambda i: (i, 0))
    return pl.pallas_call(
        functools.partial(_merge_kernel, tiles_per_moe=tiles_per_moe),
        grid=(T // tm,),
        in_specs=[rows(D_MODEL), rows(RW), rows(SW), full((1, D_MODEL)), full((D_MODEL, 2 * D_MODEL)),
                  full((1, 2 * D_MODEL)), full((RW, D_MODEL)), full((SW, D_MODEL)), full((D_MODEL, D_MODEL)),
                  full((1, D_MODEL)), full((D_MODEL, LANES)), full((D_MODEL, LANES)), full((1, LANES))],
        out_specs=[rows(D_MODEL), rows(D_MODEL), rows(LANES),
                   pl.BlockSpec((None, 8, LANES), lambda i: (i // tiles_per_moe, 0, 0))],
        out_shape=[jax.ShapeDtypeStruct((T, D_MODEL), F32), jax.ShapeDtypeStruct((T, D_MODEL), F32),
                   jax.ShapeDtypeStruct((T, LANES), F32), jax.ShapeDtypeStruct((n_moe, 8, LANES), F32)],
        scratch_shapes=[pltpu.VMEM((8, LANES), F32)],
        compiler_params=pltpu.CompilerParams(dimension_semantics=("arbitrary",),
                                             vmem_limit_bytes=VMEM_LIMIT),
        name="merge",
    )(x2, ya, yb, ng, wgt, bgt, wa, wb, wo, fg, wrh, wrl, br)


def _moe_kernel(idx_ref, gts_ref, cnt_ref, h2_ref, wg_ref, wu_ref, wd_ref, y_ref,
                start_ref, stok_ref, sg_ref, xb_ref, ob_ref):
    e = pl.program_id(1)
    tt = h2_ref.shape[0]

    @pl.when(e == 0)
    def _():
        def starts(j, acc):
            start_ref[j] = acc
            return acc + cnt_ref[0, j]
        lax.fori_loop(0, N_EXP, starts, jnp.int32(0))

        def place(j, c):
            for u in range(MOE_UNROLL):
                t = j * MOE_UNROLL + u
                for k in range(2):
                    slot = start_ref[idx_ref[0, k * tt + t]] + idx_ref[0, (2 + k) * tt + t]
                    stok_ref[slot] = t
                    sg_ref[slot] = gts_ref[0, k * tt + t]
            return c
        lax.fori_loop(0, tt // MOE_UNROLL, place, 0)
        y_ref[...] = jnp.zeros_like(y_ref)
        xb_ref[...] = jnp.zeros_like(xb_ref)

    n = cnt_ref[0, e]
    s0 = start_ref[e]
    last_slot = 2 * tt - 1

    def block(b, c):
        base = s0 + b * MOE_ROWS
        m = jnp.minimum(n - b * MOE_ROWS, MOE_ROWS)
        groups = (m + MOE_UNROLL - 1) // MOE_UNROLL

        def gather(j, c2):
            dst = xb_ref.at[pl.ds(pl.multiple_of(j * MOE_UNROLL, MOE_UNROLL), MOE_UNROLL)]
            for u in range(MOE_UNROLL):
                t = stok_ref[jnp.minimum(base + j * MOE_UNROLL + u, last_slot)]
                dst[u:u + 1, :] = h2_ref[pl.ds(t, 1), :]
            return c2
        lax.fori_loop(0, groups, gather, 0)

        xb = xb_ref[...].astype(BF16)
        hg = jnp.dot(xb, wg_ref[...], preferred_element_type=F32)
        hu = jnp.dot(xb, wu_ref[...], preferred_element_type=F32)
        hid = hg * _sigmoid(hg) * hu
        ob_ref[...] = jnp.dot(hid.astype(BF16), wd_ref[...], preferred_element_type=F32)

        def scatter(j, c2):
            done = []
            src = ob_ref.at[pl.ds(pl.multiple_of(j * MOE_UNROLL, MOE_UNROLL), MOE_UNROLL)]
            for u in range(MOE_UNROLL):
                slot = base + j * MOE_UNROLL + u
                t = stok_ref[slot]
                done.append((t, y_ref[pl.ds(t, 1), :] + sg_ref[slot] * src[u:u + 1, :]))
            for t, val in done:
                y_ref[pl.ds(t, 1), :] = val
            return c2
        full = m // MOE_UNROLL
        lax.fori_loop(0, full, scatter, 0)

        def scatter_row(r, c2):
            t = stok_ref[base + r]
            y_ref[pl.ds(t, 1), :] = y_ref[pl.ds(t, 1), :] + sg_ref[base + r] * ob_ref[pl.ds(r, 1), :]
            return c2
        lax.fori_loop(full * MOE_UNROLL, m, scatter_row, 0)
        return c

    lax.fori_loop(0, (n + MOE_ROWS - 1) // MOE_ROWS, block, 0)


def _moe_call(idx, gts, cnt, h2, wg, wu, wd, moe_tile):
    T = h2.shape[0]
    tt = moe_tile
    smem = lambda w: pl.BlockSpec((None, 1, w), lambda i, e: (i, 0, 0), memory_space=pltpu.SMEM)
    return pl.pallas_call(
        _moe_kernel,
        grid=(T // tt, N_EXP),
        in_specs=[smem(4 * tt), smem(2 * tt), smem(N_EXP),
                  pl.BlockSpec((tt, D_MODEL), lambda i, e: (i, 0), pipeline_mode=pl.Buffered(1)),
                  pl.BlockSpec((None, D_MODEL, D_EXP), lambda i, e: (e, 0, 0)),
                  pl.BlockSpec((None, D_MODEL, D_EXP), lambda i, e: (e, 0, 0)),
                  pl.BlockSpec((None, D_EXP, D_MODEL), lambda i, e: (e, 0, 0))],
        out_specs=pl.BlockSpec((tt, D_MODEL), lambda i, e: (i, 0), pipeline_mode=pl.Buffered(1)),
        out_shape=jax.ShapeDtypeStruct((T, D_MODEL), F32),
        scratch_shapes=[pltpu.SMEM((N_EXP,), jnp.int32), pltpu.SMEM((2 * tt,), jnp.int32),
                        pltpu.SMEM((2 * tt,), F32), pltpu.VMEM((MOE_ROWS, D_MODEL), F32),
                        pltpu.VMEM((MOE_ROWS, D_MODEL), F32)],
        compiler_params=pltpu.CompilerParams(dimension_semantics=("arbitrary", "arbitrary"),
                                             vmem_limit_bytes=VMEM_LIMIT),
        name="moe",
    )(idx, gts, cnt, h2, wg, wu, wd)


def _final_kernel(x1_ref, y_ref, g_ref, o_ref):
    o_ref[...] = _rms(x1_ref[...] + y_ref[...], g_ref[...])


def _final_call(x1, y, g):
    T = x1.shape[0]
    tm = 512
    rows = pl.BlockSpec((tm, D_MODEL), lambda i: (i, 0))
    return pl.pallas_call(
        _final_kernel,
        grid=(T // tm,),
        in_specs=[rows, rows, pl.BlockSpec((1, D_MODEL), lambda i: (0, 0))],
        out_specs=rows,
        out_shape=jax.ShapeDtypeStruct((T, D_MODEL), F32),
        compiler_params=pltpu.CompilerParams(dimension_semantics=("arbitrary",)),
        name="final",
    )(x1, y, g)


def _layer(x, attn_norm_g, w_in, b_gate, rwkv_mu, rwkv_w0, rwkv_w_decay, rwkv_a0, rwkv_w_a, rwkv_w_g,
           rwkv_k_k, rwkv_k_a, rwkv_r_k, rwkv_ln_g, rwkv_ln_b, w_up_rwkv, ssd_conv_w, ssd_conv_b,
           ssd_dt_bias, ssd_a_log, ssd_d, ssd_norm_g, w_up_ssd, w_out, ffn_norm_g, w_router_group,
           b_router_group, w_router_expert, b_router_expert, w_exp_gate, w_exp_up, w_exp_down):
    B, S, _ = x.shape
    T = B * S
    row = lambda a: a.reshape(1, -1).astype(F32)
    ng = row(attn_norm_g)

    w_rw = w_in[:, :RW_COLS].astype(BF16)
    wl = jnp.zeros((LANES, 2 * RW), F32)
    wl = wl.at[:DECAY_LORA, :RW].set(rwkv_w_decay).at[DECAY_LORA:, RW:].set(rwkv_w_a).astype(BF16)
    rw_args = (w_rw, row(rwkv_mu), row(rwkv_w0), wl, row(rwkv_a0), rwkv_w_g.astype(BF16),
               row(rwkv_k_k), row(rwkv_k_a), row(rwkv_r_k), row(rwkv_ln_g), row(rwkv_ln_b))

    n_ssd = 2 * SW + 2 * S_GROUPS * S_STATE + S_HEADS
    w_ssd = jnp.pad(w_in[:, RW_COLS:RW_COLS + n_ssd], ((0, 0), (0, S_COLS_PAD - n_ssd))).astype(BF16)
    pad_h = lambda a: jnp.pad(a.reshape(1, -1).astype(F32), ((0, 0), (0, LANES - S_HEADS)))
    expand = (jnp.arange(LANES)[:, None] == (jnp.arange(SW) // HEAD)[None, :]).astype(BF16)
    ssd_args = (w_ssd, ssd_conv_w.astype(F32), row(ssd_conv_b), pad_h(ssd_dt_bias), pad_h(ssd_a_log),
                row(jnp.repeat(ssd_d, HEAD)), row(ssd_norm_g), expand)
    y_a, y_b = _mix_call(x, ng, rw_args, ssd_args)

    moe_tile = min(MOE_TILE, T)
    w_gt = w_in[:, RW_COLS + n_ssd:].astype(BF16)
    w_r = jnp.zeros((D_MODEL, LANES), F32)
    w_r = w_r.at[:, :N_GROUPS].set(w_router_group).at[:, ROUTE_E0:ROUTE_E0 + N_EXP].set(w_router_expert)
    w_rh = w_r.astype(BF16)
    w_rl = (w_r - w_rh.astype(F32)).astype(BF16)
    b_r = jnp.zeros((1, LANES), F32)
    b_r = b_r.at[0, :N_GROUPS].set(b_router_group).at[0, ROUTE_E0:ROUTE_E0 + N_EXP].set(b_router_expert)
    x1, h2, route, cnt = _merge_call(
        x.reshape(T, D_MODEL), y_a.reshape(T, RW), y_b.reshape(T, SW), ng, w_gt, row(b_gate),
        w_up_rwkv.astype(BF16), w_up_ssd.astype(BF16), w_out.astype(BF16), row(ffn_norm_g), w_rh, w_rl, b_r,
        moe_tile)

    n_moe = T // moe_tile
    idx = route[:, 0:4].astype(jnp.int32).reshape(n_moe, moe_tile, 4).transpose(0, 2, 1).reshape(n_moe, 1, 4 * moe_tile)
    gts = route[:, 4:6].reshape(n_moe, moe_tile, 2).transpose(0, 2, 1).reshape(n_moe, 1, 2 * moe_tile)
    counts = cnt[:, 0:1, ROUTE_E0:ROUTE_E0 + N_EXP].astype(jnp.int32)
    y_moe = _moe_call(idx, gts, counts, h2, w_exp_gate.astype(BF16), w_exp_up.astype(BF16),
                      w_exp_down.astype(BF16), moe_tile)
    return x1, y_moe


def kernel(x, attn_norm_g, w_in, b_gate, rwkv_mu, rwkv_w0, rwkv_w_decay, rwkv_a0, rwkv_w_a, rwkv_w_g, rwkv_k_k, rwkv_k_a, rwkv_r_k, rwkv_ln_g, rwkv_ln_b, w_up_rwkv, ssd_conv_w, ssd_conv_b, ssd_dt_bias, ssd_a_log, ssd_d, ssd_norm_g, w_up_ssd, w_out, ffn_norm_g, w_router_group, b_router_group, w_router_expert, b_router_expert, w_exp_gate, w_exp_up, w_exp_down, final_norm_g):
    B, S, _ = x.shape
    depth = attn_norm_g.shape[0]
    assert depth == 1, "the final residual add is fused with the last layer's MoE output"
    layer_params = (attn_norm_g, w_in, b_gate, rwkv_mu, rwkv_w0, rwkv_w_decay, rwkv_a0, rwkv_w_a, rwkv_w_g,
                    rwkv_k_k, rwkv_k_a, rwkv_r_k, rwkv_ln_g, rwkv_ln_b, w_up_rwkv, ssd_conv_w, ssd_conv_b,
                    ssd_dt_bias, ssd_a_log, ssd_d, ssd_norm_g, w_up_ssd, w_out, ffn_norm_g, w_router_group,
                    b_router_group, w_router_expert, b_router_expert, w_exp_gate, w_exp_up, w_exp_down)
    x1, y_moe = _layer(x, *(prm[0] for prm in layer_params))
    out = _final_call(x1, y_moe, final_norm_g.reshape(1, -1).astype(F32))
    return out.reshape(B, S, D_MODEL)
```
